```python
import math
import jax
import jax.numpy as jnp
from jax import lax
import numpy as np

D_MODEL = 1024
BATCH = 8
SEQ = 8192
DEPTH = 4
DEC_BATCH = 32
DEC_SEQ = 2048
PAST_LEN = 128

N_MIXERS = 3
N_LAYERS_A = (DEPTH + 2) // 3
N_LAYERS_B = (DEPTH + 1) // 3
N_LAYERS_C = DEPTH // 3
CHUNK = 128
NORM_EPS = 1e-6
ADA_CHUNKS = 6

RET_HEADS = 4
RET_DK = D_MODEL // RET_HEADS
RET_DV = 2 * D_MODEL // RET_HEADS
RET_QK = RET_HEADS * RET_DK
RET_V = RET_HEADS * RET_DV
RET_IN = 2 * RET_QK + 2 * RET_V
RET_ROPE_BASE = 10000.0
RET_GN_EPS = 1e-5

RWKV_HEAD = 64
RWKV_HEADS = D_MODEL // RWKV_HEAD
RWKV_DECAY_LORA = 64
RWKV_ICLR_LORA = 64
RWKV_GATE_LORA = 128
RWKV_N_LERP = 6
RWKV_GN_EPS = 64e-5

SSM_INNER = 2 * D_MODEL
SSM_HEADDIM = 64
SSM_HEADS = SSM_INNER // SSM_HEADDIM
SSM_GROUPS = 4
SSM_HPG = SSM_HEADS // SSM_GROUPS
SSM_STATE = 128
SSM_CONV = 5
SSM_CONV_DIM = SSM_INNER + 2 * SSM_GROUPS * SSM_STATE
SSM_IN = SSM_INNER + SSM_CONV_DIM + 2 * SSM_HEADS

N_EXPERTS = 16
N_EXPERT_GROUPS = 4
EXPERTS_PER_GROUP = N_EXPERTS // N_EXPERT_GROUPS
GROUP_SCORE_K = 2
TOP_K = 2
EXPERT_FF = 512

kernel_name = 'hybrid_bidir_ret_rwkv7_ssd_moe_encoder'


def rms_norm(x, g):
    xf = x.astype(jnp.float32)
    y = xf * lax.rsqrt(jnp.mean(xf * xf, axis=-1, keepdims=True) + NORM_EPS)
    return (y * g).astype(x.dtype)


def head_group_norm(y, g, b, eps):
    yf = y.astype(jnp.float32)
    mu = jnp.mean(yf, axis=-1, keepdims=True)
    var = jnp.mean(jnp.square(yf - mu), axis=-1, keepdims=True)
    yn = ((yf - mu) * lax.rsqrt(var + eps)).reshape(*y.shape[:-2], -1)
    return yn * g + b


def rotary(x):
    half = x.shape[-1] // 2
    inv = RET_ROPE_BASE ** (-jnp.arange(half, dtype=jnp.float32) / half)
    ang = jnp.arange(x.shape[1], dtype=jnp.float32)[:, None] * inv[None, :]
    cos = jnp.cos(ang)[None, :, None, :]
    sin = jnp.sin(ang)[None, :, None, :]
    x1, x2 = x[..., :half], x[..., half:]
    return jnp.concatenate([x1 * cos - x2 * sin, x1 * sin + x2 * cos], axis=-1).astype(x.dtype)


def flip_t(t):
    return jnp.flip(t, axis=1)


def chunked_linear_scan(x, log_a, b_in, c_in, inclusive):
    bsz, T, G, E, P = x.shape
    N = b_in.shape[-1]
    nc = T // CHUNK

    def to_chunks(t):
        return jnp.moveaxis(t.reshape(bsz, nc, CHUNK, *t.shape[2:]), 1, 0)

    idx = jnp.arange(CHUNK)
    mask = (idx[:, None] >= idx[None, :]) if inclusive else (idx[:, None] > idx[None, :])

    def step(h, inp):
        xc, ac, bc, cc = inp
        ab = jnp.moveaxis(jnp.cumsum(ac.astype(jnp.float32), axis=1), 1, -1)
        diff = ab[..., :, None] - ab[..., None, :]
        decay = jnp.exp(jnp.where(mask, diff, -jnp.inf))
        cb = jnp.einsum('blgn,bsgn->bgls', cc, bc)
        y_diag = jnp.einsum('bgls,bgels,bsgep->blgep', cb, decay, xc)
        y_off = jnp.einsum('blgn,bgepn,bgel->blgep', cc, h, jnp.exp(ab))
        state_decay = jnp.exp(ab[..., -1:] - ab)
        h_new = (jnp.exp(ab[..., -1])[..., None, None] * h
                 + jnp.einsum('bsgn,bges,bsgep->bgepn', bc, state_decay, xc))
        return h_new, y_diag + y_off

    h0 = jnp.zeros((bsz, G, E, P, N), jnp.float32)
    _, ys = lax.scan(step, h0, (to_chunks(x), to_chunks(log_a), to_chunks(b_in), to_chunks(c_in)))
    return jnp.moveaxis(ys, 0, 1).reshape(bsz, T, G, E, P)


def retention_mixer(h, w_in, gn_g, gn_b, w_out):
    bsz, T, _ = h.shape
    proj = h @ w_in
    q, k, v, gate = jnp.split(proj, [RET_QK, 2 * RET_QK, 2 * RET_QK + RET_V], axis=-1)
    q = rotary(q.reshape(bsz, T, RET_HEADS, RET_DK))
    k = rotary(k.reshape(bsz, T, RET_HEADS, RET_DK)) * (RET_DK ** -0.5)
    v = v.reshape(bsz, T, RET_HEADS, 1, RET_DV)
    log_gamma = jnp.log(1.0 - 2.0 ** (-5.0 - jnp.arange(RET_HEADS, dtype=jnp.float32)))
    log_a = jnp.broadcast_to(log_gamma[:, None], (bsz, T, RET_HEADS, 1))
    fwd = chunked_linear_scan(v, log_a, k, q, True)
    bwd = flip_t(chunked_linear_scan(flip_t(v), log_a, flip_t(k), flip_t(q), False))
    y = head_group_norm((fwd + bwd)[:, :, :, 0], gn_g, gn_b, RET_GN_EPS)
    y = jax.nn.silu(gate) * y.astype(gate.dtype)
    return (y @ w_out).astype(h.dtype)


def rwkv7_scan(r, decay, k, v, kk, a, reverse):
    bsz, T, H, N = r.shape

    def update(S, w_t, k_t, v_t, kk_t, a_t):
        return (S * w_t[:, :, None, :]
                - jnp.einsum('bhvk,bhk->bhv', S, kk_t)[..., None] * (kk_t * a_t)[:, :, None, :]
                + v_t[..., :, None] * k_t[:, :, None, :])

    def step(S, inp):
        r_t, w_t, k_t, v_t, kk_t, a_t = inp
        if reverse:
            y = jnp.einsum('bhvk,bhk->bhv', S, r_t)
            S = update(S, w_t, k_t, v_t, kk_t, a_t)
        else:
            S = update(S, w_t, k_t, v_t, kk_t, a_t)
            y = jnp.einsum('bhvk,bhk->bhv', S, r_t)
        return S, y

    xs = tuple(jnp.moveaxis(t, 1, 0) for t in (r, decay, k, v, kk, a))
    S0 = jnp.zeros((bsz, H, N, N), jnp.float32)
    _, ys = lax.scan(step, S0, xs, reverse=reverse)
    return jnp.moveaxis(ys, 0, 1)


def rwkv7_mixer(h, mu, w_r, w_k, w_v, w_o, decay_w0, decay_w1, decay_w2,
                iclr_a0, iclr_a1, iclr_a2, gate_g1, gate_g2, k_k, k_a, r_k, gn_g, gn_b):
    bsz, T, _ = h.shape
    shp = (bsz, T, RWKV_HEADS, RWKV_HEAD)
    hp = jnp.pad(h, ((0, 0), (1, 1), (0, 0)))
    xx = 0.5 * (hp[:, :-2] + hp[:, 2:]) - h
    xr, xw, xk, xv, xa, xg = [h + xx * mu[i] for i in range(RWKV_N_LERP)]
    r = (xr @ w_r).reshape(shp)
    k = xk @ w_k
    v = (xv @ w_v).reshape(shp)
    g = jax.nn.sigmoid(xg @ gate_g1) @ gate_g2
    kkf = (k * k_k).reshape(shp).astype(jnp.float32)
    kk = kkf * lax.rsqrt(jnp.sum(kkf * kkf, axis=-1, keepdims=True) + 1e-12)
    y_sum = None
    k_fwd = None
    for d in range(2):
        w_raw = (decay_w0[d] + jnp.tanh(xw @ decay_w1[d]) @ decay_w2[d]).astype(jnp.float32)
        decay = jnp.exp(-jnp.exp(-jax.nn.softplus(-w_raw) - 0.5)).reshape(shp)
        a = jax.nn.sigmoid(iclr_a0[d] + (xa @ iclr_a1[d]) @ iclr_a2[d])
        kd = (k * (1.0 + (a - 1.0) * k_a)).reshape(shp)
        y_d = rwkv7_scan(r, decay, kd, v, kk, a.reshape(shp), reverse=(d == 1))
        y_sum = y_d if d == 0 else y_sum + y_d
        if d == 0:
            k_fwd = kd
    y = head_group_norm(y_sum, gn_g, gn_b, RWKV_GN_EPS).astype(h.dtype)
    bonus = (jnp.sum(r * k_fwd * r_k, axis=-1, keepdims=True) * v).reshape(bsz, T, D_MODEL)
    y = (y + bonus) * g
    return (y @ w_o).astype(h.dtype)


def centred_depthwise_conv(x, w, b):
    pad = SSM_CONV // 2
    out = lax.conv_general_dilated(x, w[:, None, :], window_strides=(1,), padding=[(pad, pad)],
                                   dimension_numbers=('NWC', 'WIO', 'NWC'),
                                   feature_group_count=x.shape[-1])
    return out + b


def mamba2_mixer(h, w_in, conv_w, conv_b, dt_bias, a_log, d_skip, norm_g, w_out):
    bsz, T, _ = h.shape
    proj = h @ w_in
    z, xbc, dt_raw = jnp.split(proj, [SSM_INNER, SSM_INNER + SSM_CONV_DIM], axis=-1)
    xbc = jax.nn.silu(centred_depthwise_conv(xbc, conv_w, conv_b))
    xs, b_in, c_in = jnp.split(xbc, [SSM_INNER, SSM_INNER + SSM_GROUPS * SSM_STATE], axis=-1)
    xs = xs.reshape(bsz, T, SSM_GROUPS, SSM_HPG, SSM_HEADDIM)
    b_in = b_in.reshape(bsz, T, SSM_GROUPS, SSM_STATE)
    c_in = c_in.reshape(bsz, T, SSM_GROUPS, SSM_STATE)
    dt = jax.nn.softplus(dt_raw.reshape(bsz, T, 2, SSM_HEADS).astype(jnp.float32) + dt_bias)
    log_a = dt * (-jnp.exp(a_log.astype(jnp.float32)))
    gshape = (bsz, T, SSM_GROUPS, SSM_HPG)
    dt_f, dt_b = dt[:, :, 0].reshape(gshape), dt[:, :, 1].reshape(gshape)
    la_f, la_b = log_a[:, :, 0].reshape(gshape), log_a[:, :, 1].reshape(gshape)
    fwd = chunked_linear_scan(xs * dt_f[..., None], la_f, b_in, c_in, True)
    bwd = flip_t(chunked_linear_scan(flip_t(xs * dt_b[..., None]), flip_t(la_b),
                                     flip_t(b_in), flip_t(c_in), False))
    y = fwd + bwd + xs * d_skip.reshape(SSM_GROUPS, SSM_HPG)[:, :, None]
    y = y.reshape(bsz, T, SSM_INNER) * jax.nn.silu(z)
    yg = y.reshape(bsz, T, SSM_GROUPS, SSM_INNER // SSM_GROUPS).astype(jnp.float32)
    yg = yg * lax.rsqrt(jnp.mean(yg * yg, axis=-1, keepdims=True) + NORM_EPS)
    y = (yg.reshape(bsz, T, SSM_INNER) * norm_g).astype(h.dtype)
    return (y @ w_out).astype(h.dtype)


def moe_channel_mixer(h, w_router, router_bias, w_gate, w_up, w_down):
    bsz, T, D = h.shape
    xt = h.reshape(-1, D)
    scores = jax.nn.sigmoid((xt @ w_router).astype(jnp.float32))
    sel = (scores + router_bias).reshape(-1, N_EXPERT_GROUPS, EXPERTS_PER_GROUP)
    grp_score = jnp.sum(lax.top_k(sel, GROUP_SCORE_K)[0], axis=-1)
    top_group = jnp.argmax(grp_score, axis=-1)
    in_group = jnp.arange(N_EXPERT_GROUPS)[None, :] == top_group[:, None]
    masked = jnp.where(in_group[:, :, None], sel, -jnp.inf).reshape(-1, N_EXPERTS)
    _, top_idx = lax.top_k(masked, TOP_K)
    top_s = jnp.take_along_axis(scores, top_idx, axis=-1)
    top_w = top_s / jnp.sum(top_s, axis=-1, keepdims=True)
    gates = jnp.sum(jax.nn.one_hot(top_idx, N_EXPERTS, dtype=jnp.float32) * top_w[..., None], axis=1)

    def expert(acc, inp):
        wg, wu, wd, ge = inp
        hidden = jax.nn.silu(xt @ wg) * (xt @ wu)
        return acc + ge[:, None].astype(xt.dtype) * (hidden @ wd), None

    out, _ = lax.scan(expert, jnp.zeros_like(xt), (w_gate, w_up, w_down, gates.T))
    return out.reshape(bsz, T, D)


def run_trunk(x, c, w):
    silu_c = jax.nn.silu(c)
    for i in range(DEPTH):
        mod = silu_c @ w['ada_w'][i] + w['ada_b'][i]
        sh1, sc1, g1, sh2, sc2, g2 = jnp.split(mod[:, None, :], ADA_CHUNKS, axis=-1)
        hn = rms_norm(x, w['norm_mix_g'][i]) * (1.0 + sc1) + sh1
        kind, j = i % N_MIXERS, i // N_MIXERS
        if kind == 0:
            m = retention_mixer(hn, w['ret_w_in'][j], w['ret_gn_g'][j], w['ret_gn_b'][j], w['ret_w_out'][j])
        elif kind == 1:
            m = rwkv7_mixer(hn, w['rwkv_mu'][j], w['rwkv_w_r'][j], w['rwkv_w_k'][j], w['rwkv_w_v'][j],
                            w['rwkv_w_o'][j], w['rwkv_decay_w0'][j], w['rwkv_decay_w1'][j],
                            w['rwkv_decay_w2'][j], w['rwkv_iclr_a0'][j], w['rwkv_iclr_a1'][j],
                            w['rwkv_iclr_a2'][j], w['rwkv_gate_g1'][j], w['rwkv_gate_g2'][j],
                            w['rwkv_k_k'][j], w['rwkv_k_a'][j], w['rwkv_r_k'][j],
                            w['rwkv_gn_g'][j], w['rwkv_gn_b'][j])
        else:
            m = mamba2_mixer(hn, w['ssm_w_in'][j], w['ssm_conv_w'][j], w['ssm_conv_b'][j],
                             w['ssm_dt_bias'][j], w['ssm_a_log'][j], w['ssm_d'][j],
                             w['ssm_norm_g'][j], w['ssm_w_out'][j])
        x = x + g1 * m
        hn = rms_norm(x, w['norm_ffn_g'][i]) * (1.0 + sc2) + sh2
        x = x + g2 * moe_channel_mixer(hn, w['moe_w_router'], w['moe_router_bias'],
                                       w['moe_w_gate'][i], w['moe_w_up'][i], w['moe_w_down'][i])
    return rms_norm(x, w['final_norm_g'])


def setup_inputs(seed: int = 0) -> dict:
    key = jax.random.key(seed)
    ks = iter(jax.random.split(key, 64))

    def nrm(shape, scale):
        return scale * jax.random.normal(next(ks), shape, jnp.float32)

    def unif(shape, lo, hi):
        return jax.random.uniform(next(ks), shape, jnp.float32, lo, hi)

    D = D_MODEL
    NA, NB, NC = N_LAYERS_A, N_LAYERS_B, N_LAYERS_C
    inp = {
        'x_prompt': nrm((BATCH, SEQ, D), 1.0),
        'x_sample': nrm((DEC_BATCH, DEC_SEQ, D), 1.0),
        'c_prompt': nrm((BATCH, D), 1.0),
        'c_sample': nrm((DEC_BATCH, D), 1.0),
        'ada_w': nrm((DEPTH, D, ADA_CHUNKS * D), 0.5 * D ** -0.5),
        'ada_b': nrm((DEPTH, ADA_CHUNKS * D), 0.02),
        'norm_mix_g': 1.0 + nrm((DEPTH, D), 0.05),
        'norm_ffn_g': 1.0 + nrm((DEPTH, D), 0.05),
        'final_norm_g': 1.0 + nrm((D,), 0.05),
        'ret_w_in': nrm((NA, D, RET_IN), D ** -0.5),
        'ret_gn_g': 1.0 + nrm((NA, RET_V), 0.05),
        'ret_gn_b': nrm((NA, RET_V), 0.02),
        'ret_w_out': nrm((NA, RET_V, D), RET_V ** -0.5),
        'rwkv_mu': unif((NB, RWKV_N_LERP, D), 0.0, 1.0),
        'rwkv_w_r': nrm((NB, D, D), D ** -0.5),
        'rwkv_w_k': nrm((NB, D, D), D ** -0.5),
        'rwkv_w_v': nrm((NB, D, D), D ** -0.5),
        'rwkv_w_o': nrm((NB, D, D), D ** -0.5),
        'rwkv_decay_w0': unif((NB, 2, D), -6.5, -1.5),
        'rwkv_decay_w1': nrm((NB, 2, D, RWKV_DECAY_LORA), D ** -0.5),
        'rwkv_decay_w2': nrm((NB, 2, RWKV_DECAY_LORA, D), 0.1 * RWKV_DECAY_LORA ** -0.5),
        'rwkv_iclr_a0': nrm((NB, 2, D), 0.1),
        'rwkv_iclr_a1': nrm((NB, 2, D, RWKV_ICLR_LORA), D ** -0.5),
        'rwkv_iclr_a2': nrm((NB, 2, RWKV_ICLR_LORA, D), 0.5 * RWKV_ICLR_LORA ** -0.5),
        'rwkv_gate_g1': nrm((NB, D, RWKV_GATE_LORA), D ** -0.5),
        'rwkv_gate_g2': nrm((NB, RWKV_GATE_LORA, D), RWKV_GATE_LORA ** -0.5),
        'rwkv_k_k': 0.85 + nrm((NB, D), 0.05),
        'rwkv_k_a': 1.0 + nrm((NB, D), 0.05),
        'rwkv_r_k': nrm((NB, RWKV_HEADS, RWKV_HEAD), 0.1),
        'rwkv_gn_g': 1.0 + nrm((NB, D), 0.05),
        'rwkv_gn_b': nrm((NB, D), 0.02),
        'ssm_w_in': nrm((NC, D, SSM_IN), D ** -0.5),
        'ssm_conv_w': nrm((NC, SSM_CONV, SSM_CONV_DIM), SSM_CONV ** -0.5),
        'ssm_conv_b': nrm((NC, SSM_CONV_DIM), 0.02),
    }
    dt0 = jnp.exp(unif((NC, 2, SSM_HEADS), math.log(1e-3), math.log(1e-1)))
    inp['ssm_dt_bias'] = dt0 + jnp.log(-jnp.expm1(-dt0))
    inp['ssm_a_log'] = jnp.log(unif((NC, 2, SSM_HEADS), 1.0, 16.0))
    inp['ssm_d'] = 1.0 + nrm((NC, SSM_HEADS), 0.1)
    inp['ssm_norm_g'] = 1.0 + nrm((NC, SSM_INNER), 0.05)
    inp['ssm_w_out'] = nrm((NC, SSM_INNER, D), SSM_INNER ** -0.5)
    inp['moe_w_router'] = nrm((D, N_EXPERTS), D ** -0.5)
    inp['moe_router_bias'] = nrm((N_EXPERTS,), 0.01)
    inp['moe_w_gate'] = nrm((DEPTH, N_EXPERTS, D, EXPERT_FF), D ** -0.5)
    inp['moe_w_up'] = nrm((DEPTH, N_EXPERTS, D, EXPERT_FF), D ** -0.5)
    inp['moe_w_down'] = nrm((DEPTH, N_EXPERTS, EXPERT_FF, D), EXPERT_FF ** -0.5)
    return inp


def reference(x_prompt, x_sample, c_prompt, c_sample,
              ada_w, ada_b, norm_mix_g, norm_ffn_g, final_norm_g,
              ret_w_in, ret_gn_g, ret_gn_b, ret_w_out,
              rwkv_mu, rwkv_w_r, rwkv_w_k, rwkv_w_v, rwkv_w_o,
              rwkv_decay_w0, rwkv_decay_w1, rwkv_decay_w2,
              rwkv_iclr_a0, rwkv_iclr_a1, rwkv_iclr_a2,
              rwkv_gate_g1, rwkv_gate_g2, rwkv_k_k, rwkv_k_a, rwkv_r_k, rwkv_gn_g, rwkv_gn_b,
              ssm_w_in, ssm_conv_w, ssm_conv_b, ssm_dt_bias, ssm_a_log, ssm_d, ssm_norm_g, ssm_w_out,
              moe_w_router, moe_router_bias, moe_w_gate, moe_w_up, moe_w_down):
    w = {
        'ada_w': ada_w, 'ada_b': ada_b, 'norm_mix_g': norm_mix_g, 'norm_ffn_g': norm_ffn_g,
        'final_norm_g': final_norm_g,
        'ret_w_in': ret_w_in, 'ret_gn_g': ret_gn_g, 'ret_gn_b': ret_gn_b, 'ret_w_out': ret_w_out,
        'rwkv_mu': rwkv_mu, 'rwkv_w_r': rwkv_w_r, 'rwkv_w_k': rwkv_w_k, 'rwkv_w_v': rwkv_w_v,
        'rwkv_w_o': rwkv_w_o, 'rwkv_decay_w0': rwkv_decay_w0, 'rwkv_decay_w1': rwkv_decay_w1,
        'rwkv_decay_w2': rwkv_decay_w2, 'rwkv_iclr_a0': rwkv_iclr_a0, 'rwkv_iclr_a1': rwkv_iclr_a1,
        'rwkv_iclr_a2': rwkv_iclr_a2, 'rwkv_gate_g1': rwkv_gate_g1, 'rwkv_gate_g2': rwkv_gate_g2,
        'rwkv_k_k': rwkv_k_k, 'rwkv_k_a': rwkv_k_a, 'rwkv_r_k': rwkv_r_k,
        'rwkv_gn_g': rwkv_gn_g, 'rwkv_gn_b': rwkv_gn_b,
        'ssm_w_in': ssm_w_in, 'ssm_conv_w': ssm_conv_w, 'ssm_conv_b': ssm_conv_b,
        'ssm_dt_bias': ssm_dt_bias, 'ssm_a_log': ssm_a_log, 'ssm_d': ssm_d,
        'ssm_norm_g': ssm_norm_g, 'ssm_w_out': ssm_w_out,
        'moe_w_router': moe_w_router, 'moe_router_bias': moe_router_bias,
        'moe_w_gate': moe_w_gate, 'moe_w_up': moe_w_up, 'moe_w_down': moe_w_down,
    }
    y_prompt = run_trunk(x_prompt, c_prompt, w)
    y_sample = run_trunk(x_sample, c_sample, w)
    return (y_prompt, y_sample)
```

```python
import functools
import math

import jax
import jax.numpy as jnp
from jax import lax
from jax.experimental import pallas as pl
from jax.experimental.pallas import tpu as pltpu

F32 = jnp.float32
BF16 = jnp.bfloat16
HIGHEST = lax.Precision.HIGHEST

V7X_LANES = 128
V7X_VMEM_BYTES = 64 * 1024 * 1024
VMEM_LIMIT = V7X_VMEM_BYTES - 8 * 1024 * 1024

D = 1024
DEPTH = 4
NORM_EPS = 1e-6
ADA_CHUNKS = 6

RET_HEADS = 4
RET_DK = 256
RET_DV = 512
RET_QK = RET_HEADS * RET_DK
RET_V = RET_HEADS * RET_DV
RET_IN = 2 * RET_QK + 2 * RET_V
RET_ROPE_BASE = 10000.0
RET_GN_EPS = 1e-5

RWKV_HEAD = 64
RWKV_HEADS = 16
RWKV_PAIRS = RWKV_HEADS // 2
RWKV_GN_EPS = 64e-5
RWKV_CHUNK = 64

SSM_INNER = 2048
SSM_HEADDIM = 64
SSM_HEADS = 32
SSM_PAIRS = SSM_HEADS // 2
SSM_GROUPS = 4
SSM_STATE = 128
SSM_CONV = 5
SSM_CONV_DIM = SSM_INNER + 2 * SSM_GROUPS * SSM_STATE
SSM_CHUNK = 128

N_EXPERTS = 16
N_GROUPS = 4
EXPERT_FF = 512


def _cparams(*sem):
    return pltpu.CompilerParams(dimension_semantics=sem, vmem_limit_bytes=VMEM_LIMIT)


def _sigmoid(x):
    return 1.0 / (1.0 + jnp.exp(-x))


def _silu(x):
    return x * _sigmoid(x)


def _softplus(x):
    return jnp.maximum(x, 0.0) + jnp.log(1.0 + jnp.exp(-jnp.abs(x)))


def _norm_mod(x, g, sc, sh):
    ms = jnp.mean(x * x, axis=-1, keepdims=True)
    return (x * lax.rsqrt(ms + NORM_EPS) * g) * (1.0 + sc) + sh


def _dot(a, b):
    return jnp.dot(a, b, preferred_element_type=F32)


def _dot_nt(a, b):
    return lax.dot_general(a, b, (((1,), (1,)), ((), ())), preferred_element_type=F32)


def _dot_tn(a, b):
    return lax.dot_general(a, b, (((0,), (0,)), ((), ())), preferred_element_type=F32)


def _split3(x):
    hi = x.astype(BF16)
    r1 = x - hi.astype(F32)
    mid = r1.astype(BF16)
    lo = (r1 - mid.astype(F32)).astype(BF16)
    return hi, mid, lo


def _dot_exact_lhs(a01, x):
    hi, mid, lo = _split3(x)
    return _dot(a01, hi) + _dot(a01, mid) + _dot(a01, lo)


def _dot_exact_rhs(x, b01):
    hi, mid, lo = _split3(x)
    return _dot(hi, b01) + _dot(mid, b01) + _dot(lo, b01)


def _mod_kernel(c_ref, w_ref, b_ref, o_ref):
    s = _silu(c_ref[...])
    o_ref[0] = jnp.dot(s, w_ref[0], preferred_element_type=F32, precision=HIGHEST) + b_ref[0]


def _ada_mod(c, ada_w, ada_b):
    nb = c.shape[0]
    depth, _, f = ada_w.shape
    tn = 1536
    return pl.pallas_call(
        _mod_kernel,
        out_shape=jax.ShapeDtypeStruct((depth, nb, f), F32),
        grid=(depth, f // tn),
        in_specs=[
            pl.BlockSpec((nb, D), lambda l, j: (0, 0)),
            pl.BlockSpec((1, D, tn), lambda l, j: (l, 0, j)),
            pl.BlockSpec((1, 1, tn), lambda l, j: (l, 0, j)),
        ],
        out_specs=pl.BlockSpec((1, nb, tn), lambda l, j: (l, 0, j)),
        compiler_params=_cparams("parallel", "parallel"),
        name="ada_mod",
    )(c, ada_w, ada_b.reshape(depth, 1, f))


def _proj_kernel(x_ref, g_ref, sc_ref, sh_ref, w_ref, o_ref, hn_ref):
    @pl.when(pl.program_id(2) == 0)
    def _():
        hn_ref[...] = _norm_mod(x_ref[0], g_ref[...], sc_ref[0], sh_ref[0]).astype(BF16)

    o_ref[0] = _dot(hn_ref[...], w_ref[...]).astype(o_ref.dtype)


def _norm_proj(x, g, sc, sh, w, out_dtype, tm=512, tn=1024):
    b, t, _ = x.shape
    f = w.shape[1]
    tm = min(tm, t)
    tn = min(tn, f)
    return pl.pallas_call(
        _proj_kernel,
        out_shape=jax.ShapeDtypeStruct((b, t, f), out_dtype),
        grid=(b, t // tm, f // tn),
        in_specs=[
            pl.BlockSpec((1, tm, D), lambda bi, i, j: (bi, i, 0)),
            pl.BlockSpec((1, D), lambda bi, i, j: (0, 0)),
            pl.BlockSpec((1, 1, D), lambda bi, i, j: (bi, 0, 0)),
            pl.BlockSpec((1, 1, D), lambda bi, i, j: (bi, 0, 0)),
            pl.BlockSpec((D, tn), lambda bi, i, j: (0, j)),
        ],
        out_specs=pl.BlockSpec((1, tm, tn), lambda bi, i, j: (bi, i, j)),
        scratch_shapes=[pltpu.VMEM((tm, D), BF16)],
        compiler_params=_cparams("parallel", "parallel", "arbitrary"),
        name="norm_proj",
    )(x, g.reshape(1, D), sc, sh, w)


def _normmod_kernel(x_ref, g_ref, sc_ref, sh_ref, o_ref):
    o_ref[0] = _norm_mod(x_ref[0], g_ref[...], sc_ref[0], sh_ref[0])


def _norm_only(x, g, sc, sh, tm=512):
    b, t, _ = x.shape
    tm = min(tm, t)
    return pl.pallas_call(
        _normmod_kernel,
        out_shape=jax.ShapeDtypeStruct((b, t, D), F32),
        grid=(b, t // tm),
        in_specs=[
            pl.BlockSpec((1, tm, D), lambda bi, i: (bi, i, 0)),
            pl.BlockSpec((1, D), lambda bi, i: (0, 0)),
            pl.BlockSpec((1, 1, D), lambda bi, i: (bi, 0, 0)),
            pl.BlockSpec((1, 1, D), lambda bi, i: (bi, 0, 0)),
        ],
        out_specs=pl.BlockSpec((1, tm, D), lambda bi, i: (bi, i, 0)),
        compiler_params=_cparams("parallel", "parallel"),
        name="norm_mod",
    )(x, g.reshape(1, D), sc, sh)


def _out_kernel(y_ref, w_ref, x_ref, g_ref, o_ref):
    o_ref[0] = x_ref[0] + g_ref[0] * _dot(y_ref[0], w_ref[...])


def _out_proj(y, w, x, gate, tm=512):
    b, t, k = y.shape
    tm = min(tm, t)
    return pl.pallas_call(
        _out_kernel,
        out_shape=jax.ShapeDtypeStruct((b, t, D), F32),
        grid=(b, t // tm),
        in_specs=[
            pl.BlockSpec((1, tm, k), lambda bi, i: (bi, i, 0)),
            pl.BlockSpec((k, D), lambda bi, i: (0, 0)),
            pl.BlockSpec((1, tm, D), lambda bi, i: (bi, i, 0)),
            pl.BlockSpec((1, 1, D), lambda bi, i: (bi, 0, 0)),
        ],
        out_specs=pl.BlockSpec((1, tm, D), lambda bi, i: (bi, i, 0)),
        compiler_params=_cparams("parallel", "parallel"),
        name="out_proj",
    )(y, w, x, gate)


def _final_kernel(x_ref, g_ref, o_ref):
    x = x_ref[0]
    ms = jnp.mean(x * x, axis=-1, keepdims=True)
    o_ref[0] = x * lax.rsqrt(ms + NORM_EPS) * g_ref[...]


def _final_norm(x, g, tm=512):
    b, t, _ = x.shape
    tm = min(tm, t)
    return pl.pallas_call(
        _final_kernel,
        out_shape=jax.ShapeDtypeStruct((b, t, D), F32),
        grid=(b, t // tm),
        in_specs=[
            pl.BlockSpec((1, tm, D), lambda bi, i: (bi, i, 0)),
            pl.BlockSpec((1, D), lambda bi, i: (0, 0)),
        ],
        out_specs=pl.BlockSpec((1, tm, D), lambda bi, i: (bi, i, 0)),
        compiler_params=_cparams("parallel", "parallel"),
        name="final_norm",
    )(x, g.reshape(1, D))


def _route(sel, sc):
    n_in = N_EXPERTS // N_GROUPS
    gscore = []
    for g in range(N_GROUPS):
        a, b, c, d = sel[n_in * g:n_in * g + n_in]
        m1, n1 = jnp.maximum(a, b), jnp.minimum(a, b)
        m2, n2 = jnp.maximum(c, d), jnp.minimum(c, d)
        gscore.append(jnp.maximum(m1, m2) + jnp.maximum(jnp.minimum(m1, m2), jnp.maximum(n1, n2)))
    in_group = []
    taken = None
    for g in range(N_GROUPS):
        best = None
        for o in range(g + 1, N_GROUPS):
            c = gscore[g] >= gscore[o]
            best = c if best is None else (best & c)
        if best is None:
            best = jnp.ones_like(gscore[g], dtype=jnp.bool_)
        if taken is not None:
            best = best & jnp.logical_not(taken)
        in_group.append(best)
        taken = best if taken is None else (taken | best)
    vs, ts = [], []
    for k in range(n_in):
        v = sel[3 * n_in + k]
        t = sc[3 * n_in + k]
        for g in range(N_GROUPS - 2, -1, -1):
            v = jnp.where(in_group[g], sel[n_in * g + k], v)
            t = jnp.where(in_group[g], sc[n_in * g + k], t)
        vs.append(v)
        ts.append(t)
    chosen = []
    for i in range(n_in):
        beaten = None
        for j in range(n_in):
            if j == i:
                continue
            c = (vs[j] > vs[i]) if j > i else (vs[j] >= vs[i])
            c = c.astype(F32)
            beaten = c if beaten is None else beaten + c
        chosen.append(beaten < 1.5)
    denom = None
    for k in range(n_in):
        term = jnp.where(chosen[k], ts[k], 0.0)
        denom = term if denom is None else denom + term
    gates = []
    for g in range(N_GROUPS):
        for k in range(n_in):
            gates.append(jnp.where(in_group[g] & chosen[k], ts[k] / denom, 0.0))
    return gates


def _moe_kernel(x_ref, g_ref, sc_ref, sh_ref, g2_ref, wrt_ref, rb_ref, wg_ref, wu_ref, wd_ref,
                o_ref, hn_ref, gates_ref, gt_ref):
    e = pl.program_id(2)
    tm = x_ref.shape[1]

    @pl.when(e == 0)
    def _():
        hn = _norm_mod(x_ref[0], g_ref[...], sc_ref[0], sh_ref[0])
        hn_ref[...] = hn.astype(BF16)
        logits = lax.dot_general(wrt_ref[...], hn, (((1,), (1,)), ((), ())),
                                 preferred_element_type=F32, precision=HIGHEST)
        score = _sigmoid(logits)
        sel = score + rb_ref[...]
        gates = _route([sel[i:i + 1, :] for i in range(N_EXPERTS)],
                       [score[i:i + 1, :] for i in range(N_EXPERTS)])
        gates_ref[...] = jnp.zeros_like(gates_ref)
        for i in range(N_EXPERTS):
            gates_ref[i:i + 1, :] = gates[i]
        gt_ref[...] = gates_ref[...].T
        o_ref[0] = jnp.zeros((tm, D), F32)

    hn = hn_ref[...]
    lane = lax.broadcasted_iota(jnp.int32, (tm, V7X_LANES), 1)
    col = jnp.sum(jnp.where(lane == e, gt_ref[...], 0.0), axis=1, keepdims=True)
    hid = _silu(_dot(hn, wg_ref[0])) * _dot(hn, wu_ref[0]) * col
    o_ref[0] += _dot(hid.astype(BF16), wd_ref[0])

    @pl.when(e == N_EXPERTS - 1)
    def _():
        o_ref[0] = x_ref[0] + g2_ref[0] * o_ref[0]


def _moe(x, g, sc, sh, g2, wrt, rb, wg, wu, wd, tm=1024):
    b, t, _ = x.shape
    tm = min(tm, t)
    tok = lambda bi, i, e: (bi, i, 0)
    per_b = lambda bi, i, e: (bi, 0, 0)
    per_e = lambda bi, i, e: (e, 0, 0)
    fixed = lambda bi, i, e: (0, 0)
    return pl.pallas_call(
        _moe_kernel,
        out_shape=jax.ShapeDtypeStruct((b, t, D), F32),
        grid=(b, t // tm, N_EXPERTS),
        in_specs=[
            pl.BlockSpec((1, tm, D), tok),
            pl.BlockSpec((1, D), fixed),
            pl.BlockSpec((1, 1, D), per_b),
            pl.BlockSpec((1, 1, D), per_b),
            pl.BlockSpec((1, 1, D), per_b),
            pl.BlockSpec((N_EXPERTS, D), fixed),
            pl.BlockSpec((N_EXPERTS, 1), fixed),
            pl.BlockSpec((1, D, EXPERT_FF), per_e),
            pl.BlockSpec((1, D, EXPERT_FF), per_e),
            pl.BlockSpec((1, EXPERT_FF, D), per_e),
        ],
        out_specs=pl.BlockSpec((1, tm, D), tok),
        scratch_shapes=[
            pltpu.VMEM((tm, D), BF16),
            pltpu.VMEM((V7X_LANES, tm), F32),
            pltpu.VMEM((tm, V7X_LANES), F32),
        ],
        compiler_params=_cparams("parallel", "parallel", "arbitrary"),
        name="moe",
    )(x, g.reshape(1, D), sc, sh, g2, wrt, rb, wg, wu, wd)


def _rotary(x, cos, sin):
    half = x.shape[-1] // 2
    x1, x2 = x[:, :half], x[:, half:]
    return jnp.concatenate([x1 * cos - x2 * sin, x1 * sin + x2 * cos], axis=-1)


def _ret_bwd_kernel(k_ref, v_ref, cos_ref, sin_ref, lg_ref, hb_ref, h_ref):
    c = pl.program_id(2)
    ch = k_ref.shape[1]

    @pl.when(c == 0)
    def _():
        h_ref[...] = jnp.zeros_like(h_ref)

    hb_ref[0, 0, 0] = h_ref[...]
    lg = lg_ref[0][:, :1]
    pos = lax.broadcasted_iota(jnp.int32, (ch, 1), 0).astype(F32)
    kr = _rotary(k_ref[0].astype(F32), cos_ref[...], sin_ref[...]) * (RET_DK ** -0.5)
    kw = (kr * jnp.exp(lg * (pos + 1.0))).astype(BF16)
    h_ref[...] = jnp.exp(lg * ch) * h_ref[...] + _dot_tn(kw, v_ref[0])


def _ret_fwd_kernel(q_ref, k_ref, v_ref, gate_ref, cos_ref, sin_ref, lg_ref, hb_ref, gg_ref, gb_ref,
                    o_ref, h_ref):
    c = pl.program_id(2)
    ch = q_ref.shape[1]

    @pl.when(c == 0)
    def _():
        h_ref[...] = jnp.zeros_like(h_ref)

    lg = lg_ref[0][:, :1]
    cos, sin = cos_ref[...], sin_ref[...]
    pos = lax.broadcasted_iota(jnp.int32, (ch, 1), 0).astype(F32)
    qr = _rotary(q_ref[0].astype(F32), cos, sin)
    kr = _rotary(k_ref[0].astype(F32), cos, sin) * (RET_DK ** -0.5)
    v = v_ref[0]
    ii = lax.broadcasted_iota(jnp.int32, (ch, ch), 0)
    jj = lax.broadcasted_iota(jnp.int32, (ch, ch), 1)
    dec = jnp.exp(lg * jnp.abs(ii - jj).astype(F32))
    s = _dot_nt(qr.astype(BF16), kr.astype(BF16)) * dec
    y = _dot(s.astype(BF16), v)
    q2 = jnp.concatenate([qr * jnp.exp(lg * (pos + 1.0)), qr * jnp.exp(lg * (ch - 1.0 - pos))], axis=-1)
    h2 = jnp.concatenate([h_ref[...], hb_ref[0, 0, 0]], axis=0)
    y = y + _dot(q2.astype(BF16), h2.astype(BF16))
    kw = (kr * jnp.exp(lg * (ch - 1.0 - pos))).astype(BF16)
    h_ref[...] = jnp.exp(lg * ch) * h_ref[...] + _dot_tn(kw, v)
    mu = jnp.mean(y, axis=-1, keepdims=True)
    var = jnp.mean(jnp.square(y - mu), axis=-1, keepdims=True)
    yn = (y - mu) * lax.rsqrt(var + RET_GN_EPS) * gg_ref[...] + gb_ref[...]
    o_ref[0] = (_silu(gate_ref[0].astype(F32)) * yn).astype(o_ref.dtype)


def _retention(proj, cos, sin, lg, gn_g, gn_b, ch=512):
    b, t, _ = proj.shape
    ch = min(ch, t)
    nc = t // ch
    kq = RET_QK // RET_DK
    vq = 2 * RET_QK // RET_DV
    gq = vq + RET_HEADS
    hb = pl.pallas_call(
        _ret_bwd_kernel,
        out_shape=jax.ShapeDtypeStruct((b, RET_HEADS, nc, RET_DK, RET_DV), F32),
        grid=(b, RET_HEADS, nc),
        in_specs=[
            pl.BlockSpec((1, ch, RET_DK), lambda bi, h, c: (bi, nc - 1 - c, kq + h)),
            pl.BlockSpec((1, ch, RET_DV), lambda bi, h, c: (bi, nc - 1 - c, vq + h)),
            pl.BlockSpec((ch, RET_DK // 2), lambda bi, h, c: (nc - 1 - c, 0)),
            pl.BlockSpec((ch, RET_DK // 2), lambda bi, h, c: (nc - 1 - c, 0)),
            pl.BlockSpec((1, 1, V7X_LANES), lambda bi, h, c: (h, 0, 0)),
        ],
        out_specs=pl.BlockSpec((1, 1, 1, RET_DK, RET_DV), lambda bi, h, c: (bi, h, nc - 1 - c, 0, 0)),
        scratch_shapes=[pltpu.VMEM((RET_DK, RET_DV), F32)],
        compiler_params=_cparams("parallel", "parallel", "arbitrary"),
        name="ret_bwd_state",
    )(proj, proj, cos, sin, lg)
    return pl.pallas_call(
        _ret_fwd_kernel,
        out_shape=jax.ShapeDtypeStruct((b, t, RET_V), BF16),
        grid=(b, RET_HEADS, nc),
        in_specs=[
            pl.BlockSpec((1, ch, RET_DK), lambda bi, h, c: (bi, c, h)),
            pl.BlockSpec((1, ch, RET_DK), lambda bi, h, c: (bi, c, kq + h)),
            pl.BlockSpec((1, ch, RET_DV), lambda bi, h, c: (bi, c, vq + h)),
            pl.BlockSpec((1, ch, RET_DV), lambda bi, h, c: (bi, c, gq + h)),
            pl.BlockSpec((ch, RET_DK // 2), lambda bi, h, c: (c, 0)),
            pl.BlockSpec((ch, RET_DK // 2), lambda bi, h, c: (c, 0)),
            pl.BlockSpec((1, 1, V7X_LANES), lambda bi, h, c: (h, 0, 0)),
            pl.BlockSpec((1, 1, 1, RET_DK, RET_DV), lambda bi, h, c: (bi, h, c, 0, 0)),
            pl.BlockSpec((1, RET_DV), lambda bi, h, c: (0, h)),
            pl.BlockSpec((1, RET_DV), lambda bi, h, c: (0, h)),
        ],
        out_specs=pl.BlockSpec((1, ch, RET_DV), lambda bi, h, c: (bi, c, h)),
        scratch_shapes=[pltpu.VMEM((RET_DK, RET_DV), F32)],
        compiler_params=_cparams("parallel", "parallel", "arbitrary"),
        name="ret_fwd",
    )(proj, proj, proj, proj, cos, sin, lg, hb, gn_g.reshape(1, RET_V), gn_b.reshape(1, RET_V))


def _ret_tables(t):
    half = RET_DK // 2
    inv = RET_ROPE_BASE ** (-jnp.arange(half, dtype=F32) / half)
    ang = jnp.arange(t, dtype=F32)[:, None] * inv[None, :]
    log_gamma = jnp.log(1.0 - 2.0 ** (-5.0 - jnp.arange(RET_HEADS, dtype=F32)))
    lg = jnp.broadcast_to(log_gamma[:, None, None], (RET_HEADS, 1, V7X_LANES))
    return jnp.cos(ang), jnp.sin(ang), lg


def _ret_layer(x, g, sc, sh, gate, tables, w_in, gn_g, gn_b, w_out):
    proj = _norm_proj(x, g, sc, sh, w_in.astype(BF16), BF16)
    y = _retention(proj, *tables, gn_g, gn_b)
    return _out_proj(y, w_out.astype(BF16), x, gate)


CONV_HALO = 16


def _conv_kernel(cur_ref, prev_ref, next_ref, w_ref, b_ref, o_ref):
    i = pl.program_id(1)
    tt = cur_ref.shape[1]
    cur = cur_ref[0].astype(F32)
    prev = jnp.where(i == 0, 0.0, prev_ref[0].astype(F32))
    nxt = jnp.where(i == pl.num_programs(1) - 1, 0.0, next_ref[0].astype(F32))
    ext = jnp.concatenate([prev, cur, nxt], axis=0)
    w = w_ref[...]
    acc = jnp.zeros_like(cur) + b_ref[...]
    for s in range(SSM_CONV):
        off = CONV_HALO + s - SSM_CONV // 2
        acc = acc + w[s:s + 1, :] * ext[off:off + tt]
    o_ref[0] = _silu(acc).astype(o_ref.dtype)


def _ssd_conv(zx, conv_w, conv_b, tt=512, tc=1024):
    b, t, _ = zx.shape
    tt = min(tt, t)
    off = SSM_INNER // tc
    nh = tt // CONV_HALO
    last = t // CONV_HALO - 1
    w = jnp.zeros((8, SSM_CONV_DIM), F32).at[:SSM_CONV].set(conv_w)
    return pl.pallas_call(
        _conv_kernel,
        out_shape=jax.ShapeDtypeStruct((b, t, SSM_CONV_DIM), BF16),
        grid=(b, t // tt, SSM_CONV_DIM // tc),
        in_specs=[
            pl.BlockSpec((1, tt, tc), lambda bi, i, j: (bi, i, off + j)),
            pl.BlockSpec((1, CONV_HALO, tc), lambda bi, i, j: (bi, jnp.maximum(i * nh - 1, 0), off + j)),
            pl.BlockSpec((1, CONV_HALO, tc), lambda bi, i, j: (bi, jnp.minimum((i + 1) * nh, last), off + j)),
            pl.BlockSpec((8, tc), lambda bi, i, j: (0, j)),
            pl.BlockSpec((1, tc), lambda bi, i, j: (0, j)),
        ],
        out_specs=pl.BlockSpec((1, tt, tc), lambda bi, i, j: (bi, i, j)),
        compiler_params=_cparams("parallel", "parallel", "parallel"),
        name="ssd_conv",
    )(zx, zx, zx, w, conv_b.reshape(1, SSM_CONV_DIM))


def _ssd_decays(dt_ref, bias_ref, alog_ref):
    ch = dt_ref.shape[1]
    dt = _softplus(dt_ref[0] + bias_ref[...])
    la = dt * (-jnp.exp(alog_ref[...]))
    ii = lax.broadcasted_iota(jnp.int32, (ch, ch), 0)
    jj = lax.broadcasted_iota(jnp.int32, (ch, ch), 1)
    tri = jnp.where(ii >= jj, 1.0, 0.0).astype(BF16)
    cum = _dot_exact_lhs(tri, la)
    return dt, la, cum


def _pair_cols(lo, arr, h0):
    return jnp.where(lo, arr[:, h0:h0 + 1], arr[:, h0 + 1:h0 + 2])


def _ssd_bwd_kernel(xs_ref, b_ref, dt_ref, bias_ref, alog_ref, hb_ref, h_ref):
    c = pl.program_id(1)
    ch = xs_ref.shape[1]

    @pl.when(c == 0)
    def _():
        h_ref[...] = jnp.zeros_like(h_ref)

    hb_ref[0, 0] = h_ref[...]
    dt, la, cum = _ssd_decays(dt_ref, bias_ref, alog_ref)
    wb = jnp.exp(cum - la) * dt
    etot = jnp.exp(cum[ch - 1:ch, :])
    lo = lax.broadcasted_iota(jnp.int32, (ch, V7X_LANES), 1) < SSM_HEADDIM
    for p in range(SSM_PAIRS):
        hb0 = SSM_HEADS + 2 * p
        g = p // (SSM_PAIRS // SSM_GROUPS)
        sl = slice(p * V7X_LANES, (p + 1) * V7X_LANES)
        xw = (xs_ref[0, :, sl].astype(F32) * _pair_cols(lo, wb, hb0)).astype(BF16)
        bg = b_ref[0, :, g * SSM_STATE:(g + 1) * SSM_STATE]
        h_ref[p] = _pair_cols(lo[:1], etot, hb0) * h_ref[p] + _dot_tn(bg, xw)


def _ssd_fwd_kernel(z_ref, xs_ref, b_ref, c_ref, dt_ref, bias_ref, alog_ref, hb_ref, dsk_ref, ng_ref,
                    o_ref, h_ref, y_ref):
    c = pl.program_id(1)
    ch = xs_ref.shape[1]

    @pl.when(c == 0)
    def _():
        h_ref[...] = jnp.zeros_like(h_ref)

    dt, la, cum = _ssd_decays(dt_ref, bias_ref, alog_ref)
    cbx = cum - la
    tot = cum[ch - 1:ch, :]
    cum_t, cbx_t, dt_t = cum.T, cbx.T, dt.T
    ef = jnp.exp(cum)
    eb = jnp.exp(tot - cbx)
    wf = jnp.exp(tot - cum) * dt
    etot = jnp.exp(tot)
    lo = lax.broadcasted_iota(jnp.int32, (ch, V7X_LANES), 1) < SSM_HEADDIM
    lower = (lax.broadcasted_iota(jnp.int32, (ch, ch), 0) >= lax.broadcasted_iota(jnp.int32, (ch, ch), 1))
    ppg = SSM_PAIRS // SSM_GROUPS
    for g in range(SSM_GROUPS):
        gs = slice(g * SSM_STATE, (g + 1) * SSM_STATE)
        cg = c_ref[0, :, gs]
        bg = b_ref[0, :, gs]
        cb = _dot_nt(cg, bg)
        for q in range(ppg):
            p = g * ppg + q
            sl = slice(p * V7X_LANES, (p + 1) * V7X_LANES)
            x = xs_ref[0, :, sl]
            xf = x.astype(F32)
            y = xf * dsk_ref[:, sl]
            for hh in range(2):
                hf = 2 * p + hh
                hbw = SSM_HEADS + hf
                arg = jnp.where(lower, cum[:, hf:hf + 1] - cum_t[hf:hf + 1, :],
                                cbx_t[hbw:hbw + 1, :] - cbx[:, hbw:hbw + 1])
                m = cb * jnp.exp(arg) * jnp.where(lower, dt_t[hf:hf + 1, :], dt_t[hbw:hbw + 1, :])
                xm = jnp.where(lo if hh == 0 else jnp.logical_not(lo), x, jnp.zeros_like(x))
                y = y + _dot(m.astype(BF16), xm)
            hprev = h_ref[p]
            y = y + _pair_cols(lo, ef, 2 * p) * _dot(cg, hprev.astype(BF16))
            y = y + _pair_cols(lo, eb, SSM_HEADS + 2 * p) * _dot(cg, hb_ref[0, 0, p].astype(BF16))
            y_ref[:, sl] = y * _silu(z_ref[0, :, sl].astype(F32))
            xw = (xf * _pair_cols(lo, wf, 2 * p)).astype(BF16)
            h_ref[p] = _pair_cols(lo[:1], etot, 2 * p) * hprev + _dot_tn(bg, xw)
    gw = SSM_INNER // SSM_GROUPS
    for g in range(SSM_GROUPS):
        gs = slice(g * gw, (g + 1) * gw)
        yg = y_ref[:, gs]
        ms = jnp.mean(yg * yg, axis=-1, keepdims=True)
        o_ref[0, :, gs] = (yg * lax.rsqrt(ms + NORM_EPS) * ng_ref[:, gs]).astype(o_ref.dtype)


def _ssd_scan(zx, xbc, dtr, bias, alog, dsk, ng):
    b, t, _ = xbc.shape
    ch = SSM_CHUNK
    nc = t // ch
    gn = SSM_GROUPS * SSM_STATE
    bq = SSM_INNER // gn
    vec = lambda bi, c: (0, 0)
    hb = pl.pallas_call(
        _ssd_bwd_kernel,
        out_shape=jax.ShapeDtypeStruct((b, nc, SSM_PAIRS, SSM_STATE, V7X_LANES), F32),
        grid=(b, nc),
        in_specs=[
            pl.BlockSpec((1, ch, SSM_INNER), lambda bi, c: (bi, nc - 1 - c, 0)),
            pl.BlockSpec((1, ch, gn), lambda bi, c: (bi, nc - 1 - c, bq)),
            pl.BlockSpec((1, ch, V7X_LANES), lambda bi, c: (bi, nc - 1 - c, 0)),
            pl.BlockSpec((1, V7X_LANES), vec),
            pl.BlockSpec((1, V7X_LANES), vec),
        ],
        out_specs=pl.BlockSpec((1, 1, SSM_PAIRS, SSM_STATE, V7X_LANES), lambda bi, c: (bi, nc - 1 - c, 0, 0, 0)),
        scratch_shapes=[pltpu.VMEM((SSM_PAIRS, SSM_STATE, V7X_LANES), F32)],
        compiler_params=_cparams("parallel", "arbitrary"),
        name="ssd_bwd_state",
    )(xbc, xbc, dtr, bias, alog)
    return pl.pallas_call(
        _ssd_fwd_kernel,
        out_shape=jax.ShapeDtypeStruct((b, t, SSM_INNER), BF16),
        grid=(b, nc),
        in_specs=[
            pl.BlockSpec((1, ch, SSM_INNER), lambda bi, c: (bi, c, 0)),
            pl.BlockSpec((1, ch, SSM_INNER), lambda bi, c: (bi, c, 0)),
            pl.BlockSpec((1, ch, gn), lambda bi, c: (bi, c, bq)),
            pl.BlockSpec((1, ch, gn), lambda bi, c: (bi, c, bq + 1)),
            pl.BlockSpec((1, ch, V7X_LANES), lambda bi, c: (bi, c, 0)),
            pl.BlockSpec((1, V7X_LANES), vec),
            pl.BlockSpec((1, V7X_LANES), vec),
            pl.BlockSpec((1, 1, SSM_PAIRS, SSM_STATE, V7X_LANES), lambda bi, c: (bi, c, 0, 0, 0)),
            pl.BlockSpec((1, SSM_INNER), vec),
            pl.BlockSpec((1, SSM_INNER), vec),
        ],
        out_specs=pl.BlockSpec((1, ch, SSM_INNER), lambda bi, c: (bi, c, 0)),
        scratch_shapes=[pltpu.VMEM((SSM_PAIRS, SSM_STATE, V7X_LANES), F32), pltpu.VMEM((ch, SSM_INNER), F32)],
        compiler_params=_cparams("parallel", "arbitrary"),
        name="ssd_fwd",
    )(zx, xbc, xbc, xbc, dtr, bias, alog, hb, dsk, ng)


def _pad_lanes(v):
    v = v.reshape(1, -1)
    return jnp.pad(v, ((0, 0), (0, V7X_LANES - v.shape[1])))


def _ssd_layer(x, g, sc, sh, gate, w_in, conv_w, conv_b, dt_bias, a_log, d_skip, norm_g, w_out):
    n_main = SSM_INNER + SSM_CONV_DIM
    w_main = w_in[:, :n_main].astype(BF16)
    w_dt = jnp.pad(w_in[:, n_main:], ((0, 0), (0, V7X_LANES - 2 * SSM_HEADS))).astype(BF16)
    zx = _norm_proj(x, g, sc, sh, w_main, BF16)
    dtr = _norm_proj(x, g, sc, sh, w_dt, F32)
    xbc = _ssd_conv(zx, conv_w, conv_b)
    dsk = jnp.repeat(d_skip, SSM_HEADDIM).reshape(1, SSM_INNER)
    y = _ssd_scan(zx, xbc, dtr, _pad_lanes(dt_bias), _pad_lanes(a_log), dsk, norm_g.reshape(1, SSM_INNER))
    return _out_proj(y, w_out.astype(BF16), x, gate)


def _head_sums(x, ones_bd):
    return _dot_exact_rhs(x, ones_bd)


def _rwkv_prep_kernel(h_ref, hp_ref, hx_ref, mu_ref, wr_ref, wk_ref, wv_ref, dw1_ref, dw2_ref, a1_ref, a2_ref,
                      g1_ref, g2_ref, w0_ref, a0_ref, kk_ref, ka_ref, ones_ref,
                      r_o, v_o, kn_o, g_o, lwf_o, lwb_o, kdf_o, kdb_o, bf_o, bb_o):
    i = pl.program_id(1)
    h = h_ref[0]
    tt = h.shape[0]
    prev_row = jnp.where(i == 0, 0.0, hp_ref[0][7:8, :])
    next_row = jnp.where(i == pl.num_programs(1) - 1, 0.0, hx_ref[0][0:1, :])
    row = lax.broadcasted_iota(jnp.int32, (tt, 1), 0)
    up = jnp.where(row == 0, prev_row, pltpu.roll(h, 1, axis=0))
    dn = jnp.where(row == tt - 1, next_row, pltpu.roll(h, tt - 1, axis=0))
    xx = 0.5 * (up + dn) - h
    mu = mu_ref[...]
    xr, xw, xk, xv, xa, xg = [(h + xx * mu[j:j + 1, :]).astype(BF16) for j in range(6)]
    r = _dot(xr, wr_ref[...])
    k = _dot(xk, wk_ref[...])
    v = _dot(xv, wv_ref[...])
    g = _dot(_sigmoid(_dot(xg, g1_ref[...])).astype(BF16), g2_ref[...])
    wlo = _dot(jnp.tanh(_dot(xw, dw1_ref[...])).astype(BF16), dw2_ref[...])
    alo = _dot(_dot(xa, a1_ref[...]).astype(BF16), a2_ref[...])
    kkf = k * kk_ref[...]
    ones_bd = ones_ref[...]
    kn_parts = []
    for p in range(RWKV_PAIRS):
        sl = slice(p * V7X_LANES, (p + 1) * V7X_LANES)
        kp = kkf[:, sl]
        kn_parts.append(kp * lax.rsqrt(_head_sums(kp * kp, ones_bd) + 1e-12))
    kn = jnp.concatenate(kn_parts, axis=-1)
    r_o[0] = r.astype(r_o.dtype)
    v_o[0] = v.astype(v_o.dtype)
    kn_o[0] = kn.astype(kn_o.dtype)
    g_o[0] = g.astype(g_o.dtype)
    ka = ka_ref[...]
    for d, (lw_o, kd_o, b_o) in enumerate(((lwf_o, kdf_o, bf_o), (lwb_o, kdb_o, bb_o))):
        w_raw = w0_ref[d:d + 1, :] + wlo[:, d * D:(d + 1) * D]
        lw_o[0] = -jnp.exp(-_softplus(-w_raw) - 0.5)
        a = _sigmoid(a0_ref[d:d + 1, :] + alo[:, d * D:(d + 1) * D])
        kd_o[0] = (k * (1.0 + (a - 1.0) * ka)).astype(kd_o.dtype)
        b_o[0] = (kn * a).astype(b_o.dtype)


def _block_diag2(w):
    z = jnp.zeros_like(w[0])
    return jnp.concatenate([jnp.concatenate([w[0], z], axis=1), jnp.concatenate([z, w[1]], axis=1)], axis=0)


def _rwkv_prep(hn, mu, w_r, w_k, w_v, dw0, dw1, dw2, a0, a1, a2, g1, g2, k_k, k_a, ones_bd, tt=256):
    b, t, _ = hn.shape
    tt = min(tt, t)
    nh = tt // 8
    last = t // 8 - 1
    cat2 = lambda w: jnp.concatenate([w[0], w[1]], axis=1)
    mu8 = jnp.zeros((8, D), F32).at[:mu.shape[0]].set(mu)
    ins = [hn, hn, hn, mu8, w_r.astype(BF16), w_k.astype(BF16), w_v.astype(BF16),
           cat2(dw1).astype(BF16), _block_diag2(dw2).astype(BF16), cat2(a1).astype(BF16), _block_diag2(a2).astype(BF16),
           g1.astype(BF16), g2.astype(BF16), dw0, a0, k_k.reshape(1, D), k_a.reshape(1, D), ones_bd]
    tok = lambda bi, i: (bi, i, 0)
    full = lambda a: pl.BlockSpec(a.shape, lambda bi, i: (0,) * a.ndim)
    in_specs = [
        pl.BlockSpec((1, tt, D), tok),
        pl.BlockSpec((1, 8, D), lambda bi, i: (bi, jnp.maximum(i * nh - 1, 0), 0)),
        pl.BlockSpec((1, 8, D), lambda bi, i: (bi, jnp.minimum((i + 1) * nh, last), 0)),
    ] + [full(a) for a in ins[3:]]
    sds = lambda dt: jax.ShapeDtypeStruct((b, t, D), dt)
    out_dtypes = [BF16, BF16, BF16, BF16, F32, F32, BF16, BF16, BF16, BF16]
    return pl.pallas_call(
        _rwkv_prep_kernel,
        out_shape=[sds(dt) for dt in out_dtypes],
        grid=(b, t // tt),
        in_specs=in_specs,
        out_specs=[pl.BlockSpec((1, tt, D), tok) for _ in out_dtypes],
        compiler_params=_cparams("parallel", "parallel"),
        name="rwkv_prep",
    )(*ins)


def _unit_tri_inverse(a):
    n = a.shape[0]
    eye = jnp.where(lax.broadcasted_iota(jnp.int32, (n, n), 0) == lax.broadcasted_iota(jnp.int32, (n, n), 1), 1.0, 0.0)
    mm = lambda u, w: jnp.dot(u, w, preferred_element_type=F32, precision=HIGHEST)
    x = eye - a
    pw = mm(a, a)
    k = 2
    while k < RWKV_CHUNK:
        x = x + mm(x, pw)
        k *= 2
        if k < RWKV_CHUNK:
            pw = mm(pw, pw)
    return x


def _rwkv_scan_kernel(reverse, final, r_ref, kn_ref, v_ref, kd_ref, b_ref, lw_ref, *rest):
    if final:
        yb_ref, g_ref, rk_ref, gg_ref, gb_ref, ones_ref, o_ref, s_ref = rest
    else:
        o_ref, s_ref = rest
    c = pl.program_id(1)
    ch = RWKV_CHUNK

    @pl.when(c == 0)
    def _():
        s_ref[...] = jnp.zeros_like(s_ref)

    lw = lw_ref[0]
    ii = lax.broadcasted_iota(jnp.int32, (ch, ch), 0)
    jj = lax.broadcasted_iota(jnp.int32, (ch, ch), 1)
    tri = jnp.where((ii <= jj) if reverse else (ii >= jj), 1.0, 0.0).astype(BF16)
    cum = _dot_exact_lhs(tri, lw)
    cum_ex = cum - lw
    tot = cum[0:1, :] if reverse else cum[ch - 1:ch, :]
    r = r_ref[0].astype(F32)
    kn = kn_ref[0].astype(F32)
    kd = kd_ref[0].astype(F32)
    bb = b_ref[0].astype(F32)
    e_ex = jnp.exp(cum_ex)
    e_neg = jnp.exp(-cum)
    e_end = jnp.exp(tot - cum)
    kq = kn * e_ex
    rt = r * (e_ex if reverse else jnp.exp(cum))
    kdt = kd * e_neg
    bt = bb * e_neg
    kdl = kd * e_end
    bl = bb * e_end
    etot = jnp.exp(tot)

    n2 = 2 * ch
    i2 = lax.broadcasted_iota(jnp.int32, (n2, n2), 0)
    j2 = lax.broadcasted_iota(jnp.int32, (n2, n2), 1)
    same = (i2 < ch) == (j2 < ch)
    ti, tj = i2 & (ch - 1), j2 & (ch - 1)
    strict = same & ((ti < tj) if reverse else (ti > tj))
    incl = strict if reverse else (same & (ti >= tj))
    lo = lax.broadcasted_iota(jnp.int32, (ch, V7X_LANES), 1) < RWKV_HEAD

    for p in range(RWKV_PAIRS):
        sl = slice(p * V7X_LANES, (p + 1) * V7X_LANES)

        def stk(a):
            ap = a[:, sl]
            return jnp.concatenate([jnp.where(lo, ap, 0.0), jnp.where(lo, 0.0, ap)], axis=0).astype(BF16)

        kqs, rts, kdts, bts, kdls, bls = stk(kq), stk(rt), stk(kdt), stk(bt), stk(kdl), stk(bl)
        vs = stk(v_ref[0].astype(F32))
        a1 = jnp.where(strict, _dot_nt(kqs, kdts), 0.0)
        a2 = jnp.where(strict, _dot_nt(kqs, bts), 0.0)
        a3 = jnp.where(incl, _dot_nt(rts, kdts), 0.0)
        a4 = jnp.where(incl, _dot_nt(rts, bts), 0.0)
        tinv = _unit_tri_inverse(a2)
        s = s_ref[p]
        sb = s.astype(BF16)
        rhs = _dot_nt(kqs, sb) + _dot(a1.astype(BF16), vs)
        z = jnp.dot(tinv, rhs, preferred_element_type=F32, precision=HIGHEST)
        zb = z.astype(BF16)
        ys = _dot_nt(rts, sb) + _dot(a3.astype(BF16), vs) - _dot(a4.astype(BF16), zb)
        y = ys[:ch] + ys[ch:]
        s_ref[p] = s * etot[:, sl] + _dot_tn(vs, kdls) - _dot_tn(zb, bls)
        if not final:
            o_ref[0, :, sl] = y
        else:
            ones_bd = ones_ref[...]
            ysum = y + yb_ref[0, :, sl]
            mean = _head_sums(ysum, ones_bd) * (1.0 / RWKV_HEAD)
            dev = ysum - mean
            var = _head_sums(dev * dev, ones_bd) * (1.0 / RWKV_HEAD)
            yn = dev * lax.rsqrt(var + RWKV_GN_EPS) * gg_ref[:, sl] + gb_ref[:, sl]
            bonus = _head_sums(r[:, sl] * kd[:, sl] * rk_ref[:, sl], ones_bd) * v_ref[0, :, sl].astype(F32)
            o_ref[0, :, sl] = ((yn + bonus) * g_ref[0, :, sl].astype(F32)).astype(o_ref.dtype)


def _rwkv_scan(reverse, r, kn, v, kd, bb, lw, extra=None):
    b, t, _ = r.shape
    ch = RWKV_CHUNK
    nc = t // ch
    tok = (lambda bi, c: (bi, nc - 1 - c, 0)) if reverse else (lambda bi, c: (bi, c, 0))
    vec = lambda bi, c: (0, 0)
    blk = pl.BlockSpec((1, ch, D), tok)
    ins = [r, kn, v, kd, bb, lw]
    in_specs = [blk] * 6
    final = extra is not None
    if final:
        yb, g, rk, gg, gb, ones_bd = extra
        ins += [yb, g, rk, gg, gb, ones_bd]
        in_specs += [blk, blk, pl.BlockSpec((1, D), vec), pl.BlockSpec((1, D), vec), pl.BlockSpec((1, D), vec),
                     pl.BlockSpec((V7X_LANES, V7X_LANES), vec)]
    return pl.pallas_call(
        functools.partial(_rwkv_scan_kernel, reverse, final),
        out_shape=jax.ShapeDtypeStruct((b, t, D), BF16 if final else F32),
        grid=(b, nc),
        in_specs=in_specs,
        out_specs=blk,
        scratch_shapes=[pltpu.VMEM((RWKV_PAIRS, V7X_LANES, V7X_LANES), F32)],
        compiler_params=_cparams("parallel", "arbitrary"),
        name="rwkv_scan_bwd" if reverse else "rwkv_scan_fwd",
    )(*ins)


def _rwkv_layer(x, g, sc, sh, gate, mu, w_r, w_k, w_v, w_o, dw0, dw1, dw2, a0, a1, a2, g1, g2,
                k_k, k_a, r_k, gn_g, gn_b):
    lane = jnp.arange(V7X_LANES)
    ones_bd = ((lane[:, None] < RWKV_HEAD) == (lane[None, :] < RWKV_HEAD)).astype(BF16)
    hn = _norm_only(x, g, sc, sh)
    r, v, kn, gg, lwf, lwb, kdf, kdb, bf, bb = _rwkv_prep(hn, mu, w_r, w_k, w_v, dw0, dw1, dw2, a0, a1, a2, g1, g2,
                                                          k_k, k_a, ones_bd)
    yb = _rwkv_scan(True, r, kn, v, kdb, bb, lwb)
    y = _rwkv_scan(False, r, kn, v, kdf, bf, lwf,
                   extra=(yb, gg, r_k.reshape(1, D), gn_g.reshape(1, D), gn_b.reshape(1, D), ones_bd))
    return _out_proj(y, w_o.astype(BF16), x, gate)


def _run_trunk(x, mod, w):
    tables = _ret_tables(x.shape[1])
    wrt = w['moe_w_router'].T
    rb = w['moe_router_bias'].reshape(N_EXPERTS, 1)
    for i in range(DEPTH):
        sh1, sc1, g1, sh2, sc2, g2 = [m[:, None, :] for m in jnp.split(mod[i], ADA_CHUNKS, axis=-1)]
        kind, j = i % 3, i // 3
        ng = w['norm_mix_g'][i]
        if kind == 0:
            x = _ret_layer(x, ng, sc1, sh1, g1, tables, w['ret_w_in'][j], w['ret_gn_g'][j], w['ret_gn_b'][j],
                           w['ret_w_out'][j])
        elif kind == 1:
            x = _rwkv_layer(x, ng, sc1, sh1, g1, w['rwkv_mu'][j], w['rwkv_w_r'][j], w['rwkv_w_k'][j],
                            w['rwkv_w_v'][j], w['rwkv_w_o'][j], w['rwkv_decay_w0'][j], w['rwkv_decay_w1'][j],
                            w['rwkv_decay_w2'][j], w['rwkv_iclr_a0'][j], w['rwkv_iclr_a1'][j], w['rwkv_iclr_a2'][j],
                            w['rwkv_gate_g1'][j], w['rwkv_gate_g2'][j], w['rwkv_k_k'][j], w['rwkv_k_a'][j],
                            w['rwkv_r_k'][j], w['rwkv_gn_g'][j], w['rwkv_gn_b'][j])
        else:
            x = _ssd_layer(x, ng, sc1, sh1, g1, w['ssm_w_in'][j], w['ssm_conv_w'][j], w['ssm_conv_b'][j],
                           w['ssm_dt_bias'][j], w['ssm_a_log'][j], w['ssm_d'][j], w['ssm_norm_g'][j],
                           w['ssm_w_out'][j])
        x = _moe(x, w['norm_ffn_g'][i], sc2, sh2, g2, wrt, rb, w['moe_w_gate'][i].astype(BF16),
                 w['moe_w_up'][i].astype(BF16), w['moe_w_down'][i].astype(BF16))
    return _final_norm(x, w['final_norm_g'])


def kernel(x_prompt, x_sample, c_prompt, c_sample,
           ada_w, ada_b, norm_mix_g, norm_ffn_g, final_norm_g,
           ret_w_in, ret_gn_g, ret_gn_b, ret_w_out,
           rwkv_mu, rwkv_w_r, rwkv_w_k, rwkv_w_v, rwkv_w_o,
           rwkv_decay_w0, rwkv_decay_w1, rwkv_decay_w2,
           rwkv_iclr_a0, rwkv_iclr_a1, rwkv_iclr_a2,
           rwkv_gate_g1, rwkv_gate_g2, rwkv_k_k, rwkv_k_a, rwkv_r_k, rwkv_gn_g, rwkv_gn_b,
           ssm_w_in, ssm_conv_w, ssm_conv_b, ssm_dt_bias, ssm_a_log, ssm_d, ssm_norm_g, ssm_w_out,
           moe_w_router, moe_router_bias, moe_w_gate, moe_w_up, moe_w_down):
    w = dict(
        norm_mix_g=norm_mix_g, norm_ffn_g=norm_ffn_g, final_norm_g=final_norm_g,
        ret_w_in=ret_w_in, ret_gn_g=ret_gn_g, ret_gn_b=ret_gn_b, ret_w_out=ret_w_out,
        rwkv_mu=rwkv_mu, rwkv_w_r=rwkv_w_r, rwkv_w_k=rwkv_w_k, rwkv_w_v=rwkv_w_v, rwkv_w_o=rwkv_w_o,
        rwkv_decay_w0=rwkv_decay_w0, rwkv_decay_w1=rwkv_decay_w1, rwkv_decay_w2=rwkv_decay_w2,
        rwkv_iclr_a0=rwkv_iclr_a0, rwkv_iclr_a1=rwkv_iclr_a1, rwkv_iclr_a2=rwkv_iclr_a2,
        rwkv_gate_g1=rwkv_gate_g1, rwkv_gate_g2=rwkv_gate_g2, rwkv_k_k=rwkv_k_k, rwkv_k_a=rwkv_k_a,
        rwkv_r_k=rwkv_r_k, rwkv_gn_g=rwkv_gn_g, rwkv_gn_b=rwkv_gn_b,
        ssm_w_in=ssm_w_in, ssm_conv_w=ssm_conv_w, ssm_conv_b=ssm_conv_b, ssm_dt_bias=ssm_dt_bias,
        ssm_a_log=ssm_a_log, ssm_d=ssm_d, ssm_norm_g=ssm_norm_g, ssm_w_out=ssm_w_out,
        moe_w_router=moe_w_router, moe_router_bias=moe_router_bias, moe_w_gate=moe_w_gate,
        moe_w_up=moe_w_up, moe_w_down=moe_w_down,
    )
    nb = x_prompt.shape[0]
    mod = _ada_mod(jnp.concatenate([c_prompt, c_sample], axis=0), ada_w, ada_b)
    y_prompt = _run_trunk(x_prompt, mod[:, :nb], w)
    y_sample = _run_trunk(x_sample, mod[:, nb:], w)
    return (y_prompt, y_sample)
```

```python
import functools
import math

import jax
import jax.numpy as jnp
from jax import lax
from jax.experimental import pallas as pl
from jax.experimental.pallas import tpu as pltpu

F32 = jnp.float32
BF16 = jnp.bfloat16
HIGHEST = lax.Precision.HIGHEST

V7X_LANES = 128
V7X_VMEM_BYTES = 64 * 1024 * 1024
VMEM_LIMIT = V7X_VMEM_BYTES - 8 * 1024 * 1024

D = 1024
DEPTH = 4
NORM_EPS = 1e-6
ADA_CHUNKS = 6

RET_HEADS = 4
RET_DK = 256
RET_DV = 512
RET_QK = RET_HEADS * RET_DK
RET_V = RET_HEADS * RET_DV
RET_IN = 2 * RET_QK + 2 * RET_V
RET_ROPE_BASE = 10000.0
RET_GN_EPS = 1e-5

RWKV_HEAD = 64
RWKV_HEADS = 16
RWKV_PAIRS = RWKV_HEADS // 2
RWKV_GN_EPS = 64e-5
RWKV_CHUNK = 64

SSM_INNER = 2048
SSM_HEADDIM = 64
SSM_HEADS = 32
SSM_PAIRS = SSM_HEADS // 2
SSM_GROUPS = 4
SSM_STATE = 128
SSM_CONV = 5
SSM_CONV_DIM = SSM_INNER + 2 * SSM_GROUPS * SSM_STATE
SSM_CHUNK = 128

N_EXPERTS = 16
N_GROUPS = 4
EXPERT_FF = 512


def _cparams(*sem):
    return pltpu.CompilerParams(dimension_semantics=sem, vmem_limit_bytes=VMEM_LIMIT)


def _sigmoid(x):
    return 1.0 / (1.0 + jnp.exp(-x))


def _silu(x):
    return x * _sigmoid(x)


def _softplus(x):
    return jnp.maximum(x, 0.0) + jnp.log(1.0 + jnp.exp(-jnp.abs(x)))


def _norm_mod(x, g, sc, sh):
    ms = jnp.mean(x * x, axis=-1, keepdims=True)
    return (x * lax.rsqrt(ms + NORM_EPS) * g) * (1.0 + sc) + sh


def _dot(a, b):
    return jnp.dot(a, b, preferred_element_type=F32)


def _dot_nt(a, b):
    return lax.dot_general(a, b, (((1,), (1,)), ((), ())), preferred_element_type=F32)


def _dot_tn(a, b):
    return lax.dot_general(a, b, (((0,), (0,)), ((), ())), preferred_element_type=F32)


def _split3(x):
    hi = x.astype(BF16)
    r1 = x - hi.astype(F32)
    mid = r1.astype(BF16)
    lo = (r1 - mid.astype(F32)).astype(BF16)
    return hi, mid, lo


def _dot_exact_lhs(a01, x):
    hi, mid, lo = _split3(x)
    return _dot(a01, hi) + _dot(a01, mid) + _dot(a01, lo)


def _dot_exact_rhs(x, b01):
    hi, mid, lo = _split3(x)
    return _dot(hi, b01) + _dot(mid, b01) + _dot(lo, b01)


def _mod_kernel(c_ref, w_ref, b_ref, o_ref):
    s = _silu(c_ref[...])
    o_ref[0] = jnp.dot(s, w_ref[0], preferred_element_type=F32, precision=HIGHEST) + b_ref[0]


def _ada_mod(c, ada_w, ada_b):
    nb = c.shape[0]
    depth, _, f = ada_w.shape
    tn = 1536
    return pl.pallas_call(
        _mod_kernel,
        out_shape=jax.ShapeDtypeStruct((depth, nb, f), F32),
        grid=(depth, f // tn),
        in_specs=[
            pl.BlockSpec((nb, D), lambda l, j: (0, 0)),
            pl.BlockSpec((1, D, tn), lambda l, j: (l, 0, j)),
            pl.BlockSpec((1, 1, tn), lambda l, j: (l, 0, j)),
        ],
        out_specs=pl.BlockSpec((1, nb, tn), lambda l, j: (l, 0, j)),
        compiler_params=_cparams("parallel", "parallel"),
        name="ada_mod",
    )(c, ada_w, ada_b.reshape(depth, 1, f))


def _proj_kernel(x_ref, g_ref, sc_ref, sh_ref, w_ref, o_ref, hn_ref):
    @pl.when(pl.program_id(2) == 0)
    def _():
        hn_ref[...] = _norm_mod(x_ref[0], g_ref[...], sc_ref[0], sh_ref[0]).astype(BF16)

    o_ref[0] = _dot(hn_ref[...], w_ref[...]).astype(o_ref.dtype)


def _norm_proj(x, g, sc, sh, w, out_dtype, tm=512, tn=1024):
    b, t, _ = x.shape
    f = w.shape[1]
    tm = min(tm, t)
    tn = min(tn, f)
    return pl.pallas_call(
        _proj_kernel,
        out_shape=jax.ShapeDtypeStruct((b, t, f), out_dtype),
        grid=(b, t // tm, f // tn),
        in_specs=[
            pl.BlockSpec((1, tm, D), lambda bi, i, j: (bi, i, 0)),
            pl.BlockSpec((1, D), lambda bi, i, j: (0, 0)),
            pl.BlockSpec((1, 1, D), lambda bi, i, j: (bi, 0, 0)),
            pl.BlockSpec((1, 1, D), lambda bi, i, j: (bi, 0, 0)),
            pl.BlockSpec((D, tn), lambda bi, i, j: (0, j)),
        ],
        out_specs=pl.BlockSpec((1, tm, tn), lambda bi, i, j: (bi, i, j)),
        scratch_shapes=[pltpu.VMEM((tm, D), BF16)],
        compiler_params=_cparams("parallel", "parallel", "arbitrary"),
        name="norm_proj",
    )(x, g.reshape(1, D), sc, sh, w)


def _normmod_kernel(x_ref, g_ref, sc_ref, sh_ref, o_ref):
    o_ref[0] = _norm_mod(x_ref[0], g_ref[...], sc_ref[0], sh_ref[0])


def _norm_only(x, g, sc, sh, tm=512):
    b, t, _ = x.shape
    tm = min(tm, t)
    return pl.pallas_call(
        _normmod_kernel,
        out_shape=jax.ShapeDtypeStruct((b, t, D), F32),
        grid=(b, t // tm),
        in_specs=[
            pl.BlockSpec((1, tm, D), lambda bi, i: (bi, i, 0)),
            pl.BlockSpec((1, D), lambda bi, i: (0, 0)),
            pl.BlockSpec((1, 1, D), lambda bi, i: (bi, 0, 0)),
            pl.BlockSpec((1, 1, D), lambda bi, i: (bi, 0, 0)),
        ],
        out_specs=pl.BlockSpec((1, tm, D), lambda bi, i: (bi, i, 0)),
        compiler_params=_cparams("parallel", "parallel"),
        name="norm_mod",
    )(x, g.reshape(1, D), sc, sh)


def _out_kernel(y_ref, w_ref, x_ref, g_ref, o_ref):
    o_ref[0] = x_ref[0] + g_ref[0] * _dot(y_ref[0], w_ref[...])


def _out_proj(y, w, x, gate, tm=512):
    b, t, k = y.shape
    tm = min(tm, t)
    return pl.pallas_call(
        _out_kernel,
        out_shape=jax.ShapeDtypeStruct((b, t, D), F32),
        grid=(b, t // tm),
        in_specs=[
            pl.BlockSpec((1, tm, k), lambda bi, i: (bi, i, 0)),
            pl.BlockSpec((k, D), lambda bi, i: (0, 0)),
            pl.BlockSpec((1, tm, D), lambda bi, i: (bi, i, 0)),
            pl.BlockSpec((1, 1, D), lambda bi, i: (bi, 0, 0)),
        ],
        out_specs=pl.BlockSpec((1, tm, D), lambda bi, i: (bi, i, 0)),
        compiler_params=_cparams("parallel", "parallel"),
        name="out_proj",
    )(y, w, x, gate)


def _final_kernel(x_ref, g_ref, o_ref):
    x = x_ref[0]
    ms = jnp.mean(x * x, axis=-1, keepdims=True)
    o_ref[0] = x * lax.rsqrt(ms + NORM_EPS) * g_ref[...]


def _final_norm(x, g, tm=512):
    b, t, _ = x.shape
    tm = min(tm, t)
    return pl.pallas_call(
        _final_kernel,
        out_shape=jax.ShapeDtypeStruct((b, t, D), F32),
        grid=(b, t // tm),
        in_specs=[
            pl.BlockSpec((1, tm, D), lambda bi, i: (bi, i, 0)),
            pl.BlockSpec((1, D), lambda bi, i: (0, 0)),
        ],
        out_specs=pl.BlockSpec((1, tm, D), lambda bi, i: (bi, i, 0)),
        compiler_params=_cparams("parallel", "parallel"),
        name="final_norm",
    )(x, g.reshape(1, D))


def _route(sel, sc):
    n_in = N_EXPERTS // N_GROUPS
    gscore = []
    for g in range(N_GROUPS):
        a, b, c, d = sel[n_in * g:n_in * g + n_in]
        m1, n1 = jnp.maximum(a, b), jnp.minimum(a, b)
        m2, n2 = jnp.maximum(c, d), jnp.minimum(c, d)
        gscore.append(jnp.maximum(m1, m2) + jnp.maximum(jnp.minimum(m1, m2), jnp.maximum(n1, n2)))
    in_group = []
    taken = None
    for g in range(N_GROUPS):
        best = None
        for o in range(g + 1, N_GROUPS):
            c = gscore[g] >= gscore[o]
            best = c if best is None else (best & c)
        if best is None:
            best = jnp.ones_like(gscore[g], dtype=jnp.bool_)
        if taken is not None:
            best = best & jnp.logical_not(taken)
        in_group.append(best)
        taken = best if taken is None else (taken | best)
    vs, ts = [], []
    for k in range(n_in):
        v = sel[3 * n_in + k]
        t = sc[3 * n_in + k]
        for g in range(N_GROUPS - 2, -1, -1):
            v = jnp.where(in_group[g], sel[n_in * g + k], v)
            t = jnp.where(in_group[g], sc[n_in * g + k], t)
        vs.append(v)
        ts.append(t)
    chosen = []
    for i in range(n_in):
        beaten = None
        for j in range(n_in):
            if j == i:
                continue
            c = (vs[j] > vs[i]) if j > i else (vs[j] >= vs[i])
            c = c.astype(F32)
            beaten = c if beaten is None else beaten + c
        chosen.append(beaten < 1.5)
    denom = None
    for k in range(n_in):
        term = jnp.where(chosen[k], ts[k], 0.0)
        denom = term if denom is None else denom + term
    gates = []
    for g in range(N_GROUPS):
        for k in range(n_in):
            gates.append(jnp.where(in_group[g] & chosen[k], ts[k] / denom, 0.0))
    return gates


def _moe_kernel(x_ref, g_ref, sc_ref, sh_ref, g2_ref, wrt_ref, rb_ref, wg_ref, wu_ref, wd_ref,
                o_ref, hn_ref, gates_ref, gt_ref):
    e = pl.program_id(2)
    tm = x_ref.shape[1]

    @pl.when(e == 0)
    def _():
        hn = _norm_mod(x_ref[0], g_ref[...], sc_ref[0], sh_ref[0])
        hn_ref[...] = hn.astype(BF16)
        logits = lax.dot_general(wrt_ref[...], hn, (((1,), (1,)), ((), ())),
                                 preferred_element_type=F32, precision=HIGHEST)
        score = _sigmoid(logits)
        sel = score + rb_ref[...]
        gates = _route([sel[i:i + 1, :] for i in range(N_EXPERTS)],
                       [score[i:i + 1, :] for i in range(N_EXPERTS)])
        gates_ref[...] = jnp.zeros_like(gates_ref)
        for i in range(N_EXPERTS):
            gates_ref[i:i + 1, :] = gates[i]
        gt_ref[...] = gates_ref[...].T
        o_ref[0] = jnp.zeros((tm, D), F32)

    hn = hn_ref[...]
    lane = lax.broadcasted_iota(jnp.int32, (tm, V7X_LANES), 1)
    col = jnp.sum(jnp.where(lane == e, gt_ref[...], 0.0), axis=1, keepdims=True)
    hid = _silu(_dot(hn, wg_ref[0])) * _dot(hn, wu_ref[0]) * col
    o_ref[0] += _dot(hid.astype(BF16), wd_ref[0])

    @pl.when(e == N_EXPERTS - 1)
    def _():
        o_ref[0] = x_ref[0] + g2_ref[0] * o_ref[0]


def _moe(x, g, sc, sh, g2, wrt, rb, wg, wu, wd, tm=1024):
    b, t, _ = x.shape
    tm = min(tm, t)
    tok = lambda bi, i, e: (bi, i, 0)
    per_b = lambda bi, i, e: (bi, 0, 0)
    per_e = lambda bi, i, e: (e, 0, 0)
    fixed = lambda bi, i, e: (0, 0)
    return pl.pallas_call(
        _moe_kernel,
        out_shape=jax.ShapeDtypeStruct((b, t, D), F32),
        grid=(b, t // tm, N_EXPERTS),
        in_specs=[
            pl.BlockSpec((1, tm, D), tok),
            pl.BlockSpec((1, D), fixed),
            pl.BlockSpec((1, 1, D), per_b),
            pl.BlockSpec((1, 1, D), per_b),
            pl.BlockSpec((1, 1, D), per_b),
            pl.BlockSpec((N_EXPERTS, D), fixed),
            pl.BlockSpec((N_EXPERTS, 1), fixed),
            pl.BlockSpec((1, D, EXPERT_FF), per_e),
            pl.BlockSpec((1, D, EXPERT_FF), per_e),
            pl.BlockSpec((1, EXPERT_FF, D), per_e),
        ],
        out_specs=pl.BlockSpec((1, tm, D), tok),
        scratch_shapes=[
            pltpu.VMEM((tm, D), BF16),
            pltpu.VMEM((V7X_LANES, tm), F32),
            pltpu.VMEM((tm, V7X_LANES), F32),
        ],
        compiler_params=_cparams("parallel", "parallel", "arbitrary"),
        name="moe",
    )(x, g.reshape(1, D), sc, sh, g2, wrt, rb, wg, wu, wd)


def _rotary(x, cos, sin):
    half = x.shape[-1] // 2
    x1, x2 = x[:, :half], x[:, half:]
    return jnp.concatenate([x1 * cos - x2 * sin, x1 * sin + x2 * cos], axis=-1)


def _ret_bwd_kernel(k_ref, v_ref, cos_ref, sin_ref, lg_ref, hb_ref, h_ref):
    c = pl.program_id(2)
    ch = k_ref.shape[1]

    @pl.when(c == 0)
    def _():
        h_ref[...] = jnp.zeros_like(h_ref)

    hb_ref[0, 0, 0] = h_ref[...]
    lg = lg_ref[0][:, :1]
    pos = lax.broadcasted_iota(jnp.int32, (ch, 1), 0).astype(F32)
    kr = _rotary(k_ref[0].astype(F32), cos_ref[...], sin_ref[...]) * (RET_DK ** -0.5)
    kw = (kr * jnp.exp(lg * (pos + 1.0))).astype(BF16)
    h_ref[...] = jnp.exp(lg * ch) * h_ref[...] + _dot_tn(kw, v_ref[0])


def _ret_fwd_kernel(q_ref, k_ref, v_ref, gate_ref, cos_ref, sin_ref, lg_ref, hb_ref, gg_ref, gb_ref,
                    o_ref, h_ref):
    c = pl.program_id(2)
    ch = q_ref.shape[1]

    @pl.when(c == 0)
    def _():
        h_ref[...] = jnp.zeros_like(h_ref)

    lg = lg_ref[0][:, :1]
    cos, sin = cos_ref[...], sin_ref[...]
    pos = lax.broadcasted_iota(jnp.int32, (ch, 1), 0).astype(F32)
    qr = _rotary(q_ref[0].astype(F32), cos, sin)
    kr = _rotary(k_ref[0].astype(F32), cos, sin) * (RET_DK ** -0.5)
    v = v_ref[0]
    ii = lax.broadcasted_iota(jnp.int32, (ch, ch), 0)
    jj = lax.broadcasted_iota(jnp.int32, (ch, ch), 1)
    dec = jnp.exp(lg * jnp.abs(ii - jj).astype(F32))
    s = _dot_nt(qr.astype(BF16), kr.astype(BF16)) * dec
    y = _dot(s.astype(BF16), v)
    q2 = jnp.concatenate([qr * jnp.exp(lg * (pos + 1.0)), qr * jnp.exp(lg * (ch - 1.0 - pos))], axis=-1)
    h2 = jnp.concatenate([h_ref[...], hb_ref[0, 0, 0]], axis=0)
    y = y + _dot(q2.astype(BF16), h2.astype(BF16))
    kw = (kr * jnp.exp(lg * (ch - 1.0 - pos))).astype(BF16)
    h_ref[...] = jnp.exp(lg * ch) * h_ref[...] + _dot_tn(kw, v)
    mu = jnp.mean(y, axis=-1, keepdims=True)
    var = jnp.mean(jnp.square(y - mu), axis=-1, keepdims=True)
    yn = (y - mu) * lax.rsqrt(var + RET_GN_EPS) * gg_ref[...] + gb_ref[...]
    o_ref[0] = (_silu(gate_ref[0].astype(F32)) * yn).astype(o_ref.dtype)


def _retention(proj, cos, sin, lg, gn_g, gn_b, ch=512):
    b, t, _ = proj.shape
    ch = min(ch, t)
    nc = t // ch
    kq = RET_QK // RET_DK
    vq = 2 * RET_QK // RET_DV
    gq = vq + RET_HEADS
    hb = pl.pallas_call(
        _ret_bwd_kernel,
        out_shape=jax.ShapeDtypeStruct((b, RET_HEADS, nc, RET_DK, RET_DV), F32),
        grid=(b, RET_HEADS, nc),
        in_specs=[
            pl.BlockSpec((1, ch, RET_DK), lambda bi, h, c: (bi, nc - 1 - c, kq + h)),
            pl.BlockSpec((1, ch, RET_DV), lambda bi, h, c: (bi, nc - 1 - c, vq + h)),
            pl.BlockSpec((ch, RET_DK // 2), lambda bi, h, c: (nc - 1 - c, 0)),
            pl.BlockSpec((ch, RET_DK // 2), lambda bi, h, c: (nc - 1 - c, 0)),
            pl.BlockSpec((1, 1, V7X_LANES), lambda bi, h, c: (h, 0, 0)),
        ],
        out_specs=pl.BlockSpec((1, 1, 1, RET_DK, RET_DV), lambda bi, h, c: (bi, h, nc - 1 - c, 0, 0)),
        scratch_shapes=[pltpu.VMEM((RET_DK, RET_DV), F32)],
        compiler_params=_cparams("parallel", "parallel", "arbitrary"),
        name="ret_bwd_state",
    )(proj, proj, cos, sin, lg)
    return pl.pallas_call(
        _ret_fwd_kernel,
        out_shape=jax.ShapeDtypeStruct((b, t, RET_V), BF16),
        grid=(b, RET_HEADS, nc),
        in_specs=[
            pl.BlockSpec((1, ch, RET_DK), lambda bi, h, c: (bi, c, h)),
            pl.BlockSpec((1, ch, RET_DK), lambda bi, h, c: (bi, c, kq + h)),
            pl.BlockSpec((1, ch, RET_DV), lambda bi, h, c: (bi, c, vq + h)),
            pl.BlockSpec((1, ch, RET_DV), lambda bi, h, c: (bi, c, gq + h)),
            pl.BlockSpec((ch, RET_DK // 2), lambda bi, h, c: (c, 0)),
            pl.BlockSpec((ch, RET_DK // 2), lambda bi, h, c: (c, 0)),
            pl.BlockSpec((1, 1, V7X_LANES), lambda bi, h, c: (h, 0, 0)),
            pl.BlockSpec((1, 1, 1, RET_DK, RET_DV), lambda bi, h, c: (bi, h, c, 0, 0)),
            pl.BlockSpec((1, RET_DV), lambda bi, h, c: (0, h)),
            pl.BlockSpec((1, RET_DV), lambda bi, h, c: (0, h)),
        ],
        out_specs=pl.BlockSpec((1, ch, RET_DV), lambda bi, h, c: (bi, c, h)),
        scratch_shapes=[pltpu.VMEM((RET_DK, RET_DV), F32)],
        compiler_params=_cparams("parallel", "parallel", "arbitrary"),
        name="ret_fwd",
    )(proj, proj, proj, proj, cos, sin, lg, hb, gn_g.reshape(1, RET_V), gn_b.reshape(1, RET_V))


def _ret_tables(t):
    half = RET_DK // 2
    inv = RET_ROPE_BASE ** (-jnp.arange(half, dtype=F32) / half)
    ang = jnp.arange(t, dtype=F32)[:, None] * inv[None, :]
    log_gamma = jnp.log(1.0 - 2.0 ** (-5.0 - jnp.arange(RET_HEADS, dtype=F32)))
    lg = jnp.broadcast_to(log_gamma[:, None, None], (RET_HEADS, 1, V7X_LANES))
    return jnp.cos(ang), jnp.sin(ang), lg


def _ret_layer(x, g, sc, sh, gate, tables, w_in, gn_g, gn_b, w_out):
    proj = _norm_proj(x, g, sc, sh, w_in.astype(BF16), BF16)
    y = _retention(proj, *tables, gn_g, gn_b)
    return _out_proj(y, w_out.astype(BF16), x, gate)


CONV_HALO = 16


def _conv_kernel(cur_ref, prev_ref, next_ref, w_ref, b_ref, o_ref):
    i = pl.program_id(1)
    tt = cur_ref.shape[1]
    cur = cur_ref[0].astype(F32)
    prev = jnp.where(i == 0, 0.0, prev_ref[0].astype(F32))
    nxt = jnp.where(i == pl.num_programs(1) - 1, 0.0, next_ref[0].astype(F32))
    ext = jnp.concatenate([prev, cur, nxt], axis=0)
    w = w_ref[...]
    acc = jnp.zeros_like(cur) + b_ref[...]
    for s in range(SSM_CONV):
        off = CONV_HALO + s - SSM_CONV // 2
        acc = acc + w[s:s + 1, :] * ext[off:off + tt]
    o_ref[0] = _silu(acc).astype(o_ref.dtype)


def _ssd_conv(zx, conv_w, conv_b, tt=512, tc=1024):
    b, t, _ = zx.shape
    tt = min(tt, t)
    off = SSM_INNER // tc
    nh = tt // CONV_HALO
    last = t // CONV_HALO - 1
    w = jnp.zeros((8, SSM_CONV_DIM), F32).at[:SSM_CONV].set(conv_w)
    return pl.pallas_call(
        _conv_kernel,
        out_shape=jax.ShapeDtypeStruct((b, t, SSM_CONV_DIM), BF16),
        grid=(b, t // tt, SSM_CONV_DIM // tc),
        in_specs=[
            pl.BlockSpec((1, tt, tc), lambda bi, i, j: (bi, i, off + j)),
            pl.BlockSpec((1, CONV_HALO, tc), lambda bi, i, j: (bi, jnp.maximum(i * nh - 1, 0), off + j)),
            pl.BlockSpec((1, CONV_HALO, tc), lambda bi, i, j: (bi, jnp.minimum((i + 1) * nh, last), off + j)),
            pl.BlockSpec((8, tc), lambda bi, i, j: (0, j)),
            pl.BlockSpec((1, tc), lambda bi, i, j: (0, j)),
        ],
        out_specs=pl.BlockSpec((1, tt, tc), lambda bi, i, j: (bi, i, j)),
        compiler_params=_cparams("parallel", "parallel", "parallel"),
        name="ssd_conv",
    )(zx, zx, zx, w, conv_b.reshape(1, SSM_CONV_DIM))


def _ssd_decays(dt_ref, bias_ref, alog_ref):
    ch = dt_ref.shape[1]
    dt = _softplus(dt_ref[0] + bias_ref[...])
    la = dt * (-jnp.exp(alog_ref[...]))
    ii = lax.broadcasted_iota(jnp.int32, (ch, ch), 0)
    jj = lax.broadcasted_iota(jnp.int32, (ch, ch), 1)
    tri = jnp.where(ii >= jj, 1.0, 0.0).astype(BF16)
    cum = _dot_exact_lhs(tri, la)
    return dt, la, cum


def _pair_cols(lo, arr, h0):
    return jnp.where(lo, arr[:, h0:h0 + 1], arr[:, h0 + 1:h0 + 2])


def _ssd_bwd_kernel(xs_ref, b_ref, dt_ref, bias_ref, alog_ref, hb_ref, h_ref):
    c = pl.program_id(1)
    ch = xs_ref.shape[1]

    @pl.when(c == 0)
    def _():
        h_ref[...] = jnp.zeros_like(h_ref)

    hb_ref[0, 0] = h_ref[...]
    dt, la, cum = _ssd_decays(dt_ref, bias_ref, alog_ref)
    wb = jnp.exp(cum - la) * dt
    etot = jnp.exp(cum[ch - 1:ch, :])
    lo = lax.broadcasted_iota(jnp.int32, (ch, V7X_LANES), 1) < SSM_HEADDIM
    for p in range(SSM_PAIRS):
        hb0 = SSM_HEADS + 2 * p
        g = p // (SSM_PAIRS // SSM_GROUPS)
        sl = slice(p * V7X_LANES, (p + 1) * V7X_LANES)
        xw = (xs_ref[0, :, sl].astype(F32) * _pair_cols(lo, wb, hb0)).astype(BF16)
        bg = b_ref[0, :, g * SSM_STATE:(g + 1) * SSM_STATE]
        h_ref[p] = _pair_cols(lo[:1], etot, hb0) * h_ref[p] + _dot_tn(bg, xw)


def _ssd_fwd_kernel(z_ref, xs_ref, b_ref, c_ref, dt_ref, bias_ref, alog_ref, hb_ref, dsk_ref, ng_ref,
                    o_ref, h_ref, y_ref):
    c = pl.program_id(1)
    ch = xs_ref.shape[1]

    @pl.when(c == 0)
    def _():
        h_ref[...] = jnp.zeros_like(h_ref)

    dt, la, cum = _ssd_decays(dt_ref, bias_ref, alog_ref)
    cbx = cum - la
    tot = cum[ch - 1:ch, :]
    cum_t, cbx_t, dt_t = cum.T, cbx.T, dt.T
    ef = jnp.exp(cum)
    eb = jnp.exp(tot - cbx)
    wf = jnp.exp(tot - cum) * dt
    etot = jnp.exp(tot)
    lo = lax.broadcasted_iota(jnp.int32, (ch, V7X_LANES), 1) < SSM_HEADDIM
    lower = (lax.broadcasted_iota(jnp.int32, (ch, ch), 0) >= lax.broadcasted_iota(jnp.int32, (ch, ch), 1))
    ppg = SSM_PAIRS // SSM_GROUPS
    for g in range(SSM_GROUPS):
        gs = slice(g * SSM_STATE, (g + 1) * SSM_STATE)
        cg = c_ref[0, :, gs]
        bg = b_ref[0, :, gs]
        cb = _dot_nt(cg, bg)
        for q in range(ppg):
            p = g * ppg + q
            sl = slice(p * V7X_LANES, (p + 1) * V7X_LANES)
            x = xs_ref[0, :, sl]
            xf = x.astype(F32)
            y = xf * dsk_ref[:, sl]
            for hh in range(2):
                hf = 2 * p + hh
                hbw = SSM_HEADS + hf
                arg = jnp.where(lower, cum[:, hf:hf + 1] - cum_t[hf:hf + 1, :],
                                cbx_t[hbw:hbw + 1, :] - cbx[:, hbw:hbw + 1])
                m = cb * jnp.exp(arg) * jnp.where(lower, dt_t[hf:hf + 1, :], dt_t[hbw:hbw + 1, :])
                xm = jnp.where(lo if hh == 0 else jnp.logical_not(lo), x, jnp.zeros_like(x))
                y = y + _dot(m.astype(BF16), xm)
            hprev = h_ref[p]
            y = y + _pair_cols(lo, ef, 2 * p) * _dot(cg, hprev.astype(BF16))
            y = y + _pair_cols(lo, eb, SSM_HEADS + 2 * p) * _dot(cg, hb_ref[0, 0, p].astype(BF16))
            y_ref[:, sl] = y * _silu(z_ref[0, :, sl].astype(F32))
            xw = (xf * _pair_cols(lo, wf, 2 * p)).astype(BF16)
            h_ref[p] = _pair_cols(lo[:1], etot, 2 * p) * hprev + _dot_tn(bg, xw)
    gw = SSM_INNER // SSM_GROUPS
    for g in range(SSM_GROUPS):
        gs = slice(g * gw, (g + 1) * gw)
        yg = y_ref[:, gs]
        ms = jnp.mean(yg * yg, axis=-1, keepdims=True)
        o_ref[0, :, gs] = (yg * lax.rsqrt(ms + NORM_EPS) * ng_ref[:, gs]).astype(o_ref.dtype)


def _ssd_scan(zx, xbc, dtr, bias, alog, dsk, ng):
    b, t, _ = xbc.shape
    ch = SSM_CHUNK
    nc = t // ch
    gn = SSM_GROUPS * SSM_STATE
    bq = SSM_INNER // gn
    vec = lambda bi, c: (0, 0)
    hb = pl.pallas_call(
        _ssd_bwd_kernel,
        out_shape=jax.ShapeDtypeStruct((b, nc, SSM_PAIRS, SSM_STATE, V7X_LANES), F32),
        grid=(b, nc),
        in_specs=[
            pl.BlockSpec((1, ch, SSM_INNER), lambda bi, c: (bi, nc - 1 - c, 0)),
            pl.BlockSpec((1, ch, gn), lambda bi, c: (bi, nc - 1 - c, bq)),
            pl.BlockSpec((1, ch, V7X_LANES), lambda bi, c: (bi, nc - 1 - c, 0)),
            pl.BlockSpec((1, V7X_LANES), vec),
            pl.BlockSpec((1, V7X_LANES), vec),
        ],
        out_specs=pl.BlockSpec((1, 1, SSM_PAIRS, SSM_STATE, V7X_LANES), lambda bi, c: (bi, nc - 1 - c, 0, 0, 0)),
        scratch_shapes=[pltpu.VMEM((SSM_PAIRS, SSM_STATE, V7X_LANES), F32)],
        compiler_params=_cparams("parallel", "arbitrary"),
        name="ssd_bwd_state",
    )(xbc, xbc, dtr, bias, alog)
    return pl.pallas_call(
        _ssd_fwd_kernel,
        out_shape=jax.ShapeDtypeStruct((b, t, SSM_INNER), BF16),
        grid=(b, nc),
        in_specs=[
            pl.BlockSpec((1, ch, SSM_INNER), lambda bi, c: (bi, c, 0)),
            pl.BlockSpec((1, ch, SSM_INNER), lambda bi, c: (bi, c, 0)),
            pl.BlockSpec((1, ch, gn), lambda bi, c: (bi, c, bq)),
            pl.BlockSpec((1, ch, gn), lambda bi, c: (bi, c, bq + 1)),
            pl.BlockSpec((1, ch, V7X_LANES), lambda bi, c: (bi, c, 0)),
            pl.BlockSpec((1, V7X_LANES), vec),
            pl.BlockSpec((1, V7X_LANES), vec),
            pl.BlockSpec((1, 1, SSM_PAIRS, SSM_STATE, V7X_LANES), lambda bi, c: (bi, c, 0, 0, 0)),
            pl.BlockSpec((1, SSM_INNER), vec),
            pl.BlockSpec((1, SSM_INNER), vec),
        ],
        out_specs=pl.BlockSpec((1, ch, SSM_INNER), lambda bi, c: (bi, c, 0)),
        scratch_shapes=[pltpu.VMEM((SSM_PAIRS, SSM_STATE, V7X_LANES), F32), pltpu.VMEM((ch, SSM_INNER), F32)],
        compiler_params=_cparams("parallel", "arbitrary"),
        name="ssd_fwd",
    )(zx, xbc, xbc, xbc, dtr, bias, alog, hb, dsk, ng)


def _pad_lanes(v):
    v = v.reshape(1, -1)
    return jnp.pad(v, ((0, 0), (0, V7X_LANES - v.shape[1])))


def _ssd_layer(x, g, sc, sh, gate, w_in, conv_w, conv_b, dt_bias, a_log, d_skip, norm_g, w_out):
    n_main = SSM_INNER + SSM_CONV_DIM
    w_main = w_in[:, :n_main].astype(BF16)
    w_dt = jnp.pad(w_in[:, n_main:], ((0, 0), (0, V7X_LANES - 2 * SSM_HEADS))).astype(BF16)
    zx = _norm_proj(x, g, sc, sh, w_main, BF16)
    dtr = _norm_proj(x, g, sc, sh, w_dt, F32)
    xbc = _ssd_conv(zx, conv_w, conv_b)
    dsk = jnp.repeat(d_skip, SSM_HEADDIM).reshape(1, SSM_INNER)
    y = _ssd_scan(zx, xbc, dtr, _pad_lanes(dt_bias), _pad_lanes(a_log), dsk, norm_g.reshape(1, SSM_INNER))
    return _out_proj(y, w_out.astype(BF16), x, gate)


def _head_sums(x, ones_bd):
    return _dot_exact_rhs(x, ones_bd)


def _rwkv_prep_kernel(h_ref, hp_ref, hx_ref, mu_ref, wr_ref, wk_ref, wv_ref, dw1_ref, dw2_ref, a1_ref, a2_ref,
                      g1_ref, g2_ref, w0_ref, a0_ref, kk_ref, ka_ref, ones_ref,
                      r_o, v_o, kn_o, g_o, lwf_o, lwb_o, kdf_o, kdb_o, bf_o, bb_o):
    i = pl.program_id(1)
    h = h_ref[0]
    tt = h.shape[0]
    prev_row = jnp.where(i == 0, 0.0, hp_ref[0][7:8, :])
    next_row = jnp.where(i == pl.num_programs(1) - 1, 0.0, hx_ref[0][0:1, :])
    row = lax.broadcasted_iota(jnp.int32, (tt, 1), 0)
    up = jnp.where(row == 0, prev_row, pltpu.roll(h, 1, axis=0))
    dn = jnp.where(row == tt - 1, next_row, pltpu.roll(h, tt - 1, axis=0))
    xx = 0.5 * (up + dn) - h
    mu = mu_ref[...]
    xr, xw, xk, xv, xa, xg = [(h + xx * mu[j:j + 1, :]).astype(BF16) for j in range(6)]
    r = _dot(xr, wr_ref[...])
    k = _dot(xk, wk_ref[...])
    v = _dot(xv, wv_ref[...])
    g = _dot(_sigmoid(_dot(xg, g1_ref[...])).astype(BF16), g2_ref[...])
    wlo = _dot(jnp.tanh(_dot(xw, dw1_ref[...])).astype(BF16), dw2_ref[...])
    alo = _dot(_dot(xa, a1_ref[...]).astype(BF16), a2_ref[...])
    kkf = k * kk_ref[...]
    ones_bd = ones_ref[...]
    kn_parts = []
    for p in range(RWKV_PAIRS):
        sl = slice(p * V7X_LANES, (p + 1) * V7X_LANES)
        kp = kkf[:, sl]
        kn_parts.append(kp * lax.rsqrt(_head_sums(kp * kp, ones_bd) + 1e-12))
    kn = jnp.concatenate(kn_parts, axis=-1)
    r_o[0] = r.astype(r_o.dtype)
    v_o[0] = v.astype(v_o.dtype)
    kn_o[0] = kn.astype(kn_o.dtype)
    g_o[0] = g.astype(g_o.dtype)
    ka = ka_ref[...]
    for d, (lw_o, kd_o, b_o) in enumerate(((lwf_o, kdf_o, bf_o), (lwb_o, kdb_o, bb_o))):
        w_raw = w0_ref[d:d + 1, :] + wlo[:, d * D:(d + 1) * D]
        lw_o[0] = -jnp.exp(-_softplus(-w_raw) - 0.5)
        a = _sigmoid(a0_ref[d:d + 1, :] + alo[:, d * D:(d + 1) * D])
        kd_o[0] = (k * (1.0 + (a - 1.0) * ka)).astype(kd_o.dtype)
        b_o[0] = (kn * a).astype(b_o.dtype)


def _block_diag2(w):
    z = jnp.zeros_like(w[0])
    return jnp.concatenate([jnp.concatenate([w[0], z], axis=1), jnp.concatenate([z, w[1]], axis=1)], axis=0)


def _rwkv_prep(hn, mu, w_r, w_k, w_v, dw0, dw1, dw2, a0, a1, a2, g1, g2, k_k, k_a, ones_bd, tt=256):
    b, t, _ = hn.shape
    tt = min(tt, t)
    nh = tt // 8
    last = t // 8 - 1
    cat2 = lambda w: jnp.concatenate([w[0], w[1]], axis=1)
    mu8 = jnp.zeros((8, D), F32).at[:mu.shape[0]].set(mu)
    ins = [hn, hn, hn, mu8, w_r.astype(BF16), w_k.astype(BF16), w_v.astype(BF16),
           cat2(dw1).astype(BF16), _block_diag2(dw2).astype(BF16), cat2(a1).astype(BF16), _block_diag2(a2).astype(BF16),
           g1.astype(BF16), g2.astype(BF16), dw0, a0, k_k.reshape(1, D), k_a.reshape(1, D), ones_bd]
    tok = lambda bi, i: (bi, i, 0)
    full = lambda a: pl.BlockSpec(a.shape, lambda bi, i: (0,) * a.ndim)
    in_specs = [
        pl.BlockSpec((1, tt, D), tok),
        pl.BlockSpec((1, 8, D), lambda bi, i: (bi, jnp.maximum(i * nh - 1, 0), 0)),
        pl.BlockSpec((1, 8, D), lambda bi, i: (bi, jnp.minimum((i + 1) * nh, last), 0)),
    ] + [full(a) for a in ins[3:]]
    sds = lambda dt: jax.ShapeDtypeStruct((b, t, D), dt)
    out_dtypes = [BF16, BF16, BF16, BF16, F32, F32, BF16, BF16, BF16, BF16]
    return pl.pallas_call(
        _rwkv_prep_kernel,
        out_shape=[sds(dt) for dt in out_dtypes],
        grid=(b, t // tt),
        in_specs=in_specs,
        out_specs=[pl.BlockSpec((1, tt, D), tok) for _ in out_dtypes],
        compiler_params=_cparams("parallel", "parallel"),
        name="rwkv_prep",
    )(*ins)


def _unit_tri_inverse(a_list):
    n = a_list[0].shape[0]
    eye = jnp.where(lax.broadcasted_iota(jnp.int32, (n, n), 0) == lax.broadcasted_iota(jnp.int32, (n, n), 1), 1.0, 0.0)
    mm = lambda u, w: _dot(u.astype(BF16), w.astype(BF16))
    xs = [eye - a for a in a_list]
    pws = [mm(a, a) for a in a_list]
    k = 2
    while k < RWKV_CHUNK:
        xs = [x + mm(x, pw) for x, pw in zip(xs, pws)]
        k *= 2
        if k < RWKV_CHUNK:
            pws = [mm(pw, pw) for pw in pws]
    return xs


RWKV_PAIR_GROUP = 8


def _rwkv_scan_kernel(reverse, final, r_ref, kn_ref, v_ref, kd_ref, b_ref, lw_ref, *rest):
    if final:
        yb_ref, g_ref, rk_ref, gg_ref, gb_ref, ones_ref, o_ref, s_ref = rest
    else:
        o_ref, s_ref = rest
    c = pl.program_id(1)
    ch = RWKV_CHUNK

    @pl.when(c == 0)
    def _():
        s_ref[...] = jnp.zeros_like(s_ref)

    lw = lw_ref[0]
    ii = lax.broadcasted_iota(jnp.int32, (ch, ch), 0)
    jj = lax.broadcasted_iota(jnp.int32, (ch, ch), 1)
    tri = jnp.where((ii <= jj) if reverse else (ii >= jj), 1.0, 0.0).astype(BF16)
    cum = _dot_exact_lhs(tri, lw)
    cum_ex = cum - lw
    tot = cum[0:1, :] if reverse else cum[ch - 1:ch, :]
    r = r_ref[0].astype(F32)
    kn = kn_ref[0].astype(F32)
    kd = kd_ref[0].astype(F32)
    bb = b_ref[0].astype(F32)
    e_ex = jnp.exp(cum_ex)
    e_neg = jnp.exp(-cum)
    e_end = jnp.exp(tot - cum)
    kq = kn * e_ex
    rt = r * (e_ex if reverse else jnp.exp(cum))
    kdt = kd * e_neg
    bt = bb * e_neg
    kdl = kd * e_end
    bl = bb * e_end
    etot = jnp.exp(tot)

    n2 = 2 * ch
    i2 = lax.broadcasted_iota(jnp.int32, (n2, n2), 0)
    j2 = lax.broadcasted_iota(jnp.int32, (n2, n2), 1)
    same = (i2 < ch) == (j2 < ch)
    ti, tj = i2 & (ch - 1), j2 & (ch - 1)
    strict = same & ((ti < tj) if reverse else (ti > tj))
    incl = strict if reverse else (same & (ti >= tj))
    lo = lax.broadcasted_iota(jnp.int32, (ch, V7X_LANES), 1) < RWKV_HEAD

    v = v_ref[0].astype(F32)

    def stk(a, sl):
        ap = a[:, sl]
        return jnp.concatenate([jnp.where(lo, ap, 0.0), jnp.where(lo, 0.0, ap)], axis=0).astype(BF16)

    for p0 in range(0, RWKV_PAIRS, RWKV_PAIR_GROUP):
        ps = list(range(p0, p0 + RWKV_PAIR_GROUP))
        sls = [slice(p * V7X_LANES, (p + 1) * V7X_LANES) for p in ps]
        kqs = [stk(kq, sl) for sl in sls]
        bts = [stk(bt, sl) for sl in sls]
        a2 = [jnp.where(strict, _dot_nt(x, y), 0.0) for x, y in zip(kqs, bts)]
        tinv = _unit_tri_inverse(a2)
        kdts = [stk(kdt, sl) for sl in sls]
        vs = [stk(v, sl) for sl in sls]
        a1 = [jnp.where(strict, _dot_nt(x, y), 0.0).astype(BF16) for x, y in zip(kqs, kdts)]
        ss = [s_ref[p] for p in ps]
        sbs = [s.astype(BF16) for s in ss]
        rhs = [_dot_nt(x, sb) + _dot(a, vv) for x, sb, a, vv in zip(kqs, sbs, a1, vs)]
        zbs = [_dot(t.astype(BF16), q.astype(BF16)).astype(BF16) for t, q in zip(tinv, rhs)]
        rts = [stk(rt, sl) for sl in sls]
        a3 = [jnp.where(incl, _dot_nt(x, y), 0.0).astype(BF16) for x, y in zip(rts, kdts)]
        a4 = [jnp.where(incl, _dot_nt(x, y), 0.0).astype(BF16) for x, y in zip(rts, bts)]
        yss = [_dot_nt(x, sb) + _dot(a, vv) - _dot(aa, zb)
               for x, sb, a, vv, aa, zb in zip(rts, sbs, a3, vs, a4, zbs)]
        kdls = [stk(kdl, sl) for sl in sls]
        bls = [stk(bl, sl) for sl in sls]
        for p, sl, s, vv, kk, zb, bq in zip(ps, sls, ss, vs, kdls, zbs, bls):
            s_ref[p] = s * etot[:, sl] + _dot_tn(vv, kk) - _dot_tn(zb, bq)
        for sl, ys in zip(sls, yss):
            y = ys[:ch] + ys[ch:]
            if not final:
                o_ref[0, :, sl] = y
            else:
                ones_bd = ones_ref[...]
                ysum = y + yb_ref[0, :, sl]
                mean = _head_sums(ysum, ones_bd) * (1.0 / RWKV_HEAD)
                dev = ysum - mean
                var = _head_sums(dev * dev, ones_bd) * (1.0 / RWKV_HEAD)
                yn = dev * lax.rsqrt(var + RWKV_GN_EPS) * gg_ref[:, sl] + gb_ref[:, sl]
                bonus = _head_sums(r[:, sl] * kd[:, sl] * rk_ref[:, sl], ones_bd) * v[:, sl]
                o_ref[0, :, sl] = ((yn + bonus) * g_ref[0, :, sl].astype(F32)).astype(o_ref.dtype)


def _rwkv_scan(reverse, r, kn, v, kd, bb, lw, extra=None):
    b, t, _ = r.shape
    ch = RWKV_CHUNK
    nc = t // ch
    tok = (lambda bi, c: (bi, nc - 1 - c, 0)) if reverse else (lambda bi, c: (bi, c, 0))
    vec = lambda bi, c: (0, 0)
    blk = pl.BlockSpec((1, ch, D), tok)
    ins = [r, kn, v, kd, bb, lw]
    in_specs = [blk] * 6
    final = extra is not None
    if final:
        yb, g, rk, gg, gb, ones_bd = extra
        ins += [yb, g, rk, gg, gb, ones_bd]
        in_specs += [blk, blk, pl.BlockSpec((1, D), vec), pl.BlockSpec((1, D), vec), pl.BlockSpec((1, D), vec),
                     pl.BlockSpec((V7X_LANES, V7X_LANES), vec)]
    return pl.pallas_call(
        functools.partial(_rwkv_scan_kernel, reverse, final),
        out_shape=jax.ShapeDtypeStruct((b, t, D), BF16 if final else F32),
        grid=(b, nc),
        in_specs=in_specs,
        out_specs=blk,
        scratch_shapes=[pltpu.VMEM((RWKV_PAIRS, V7X_LANES, V7X_LANES), F32)],
        compiler_params=_cparams("parallel", "arbitrary"),
        name="rwkv_scan_bwd" if reverse else "rwkv_scan_fwd",
    )(*ins)


def _rwkv_layer(x, g, sc, sh, gate, mu, w_r, w_k, w_v, w_o, dw0, dw1, dw2, a0, a1, a2, g1, g2,
                k_k, k_a, r_k, gn_g, gn_b):
    lane = jnp.arange(V7X_LANES)
    ones_bd = ((lane[:, None] < RWKV_HEAD) == (lane[None, :] < RWKV_HEAD)).astype(BF16)
    hn = _norm_only(x, g, sc, sh)
    r, v, kn, gg, lwf, lwb, kdf, kdb, bf, bb = _rwkv_prep(hn, mu, w_r, w_k, w_v, dw0, dw1, dw2, a0, a1, a2, g1, g2,
                                                          k_k, k_a, ones_bd)
    yb = _rwkv_scan(True, r, kn, v, kdb, bb, lwb)
    y = _rwkv_scan(False, r, kn, v, kdf, bf, lwf,
                   extra=(yb, gg, r_k.reshape(1, D), gn_g.reshape(1, D), gn_b.reshape(1, D), ones_bd))
    return _out_proj(y, w_o.astype(BF16), x, gate)


def _run_trunk(x, mod, w):
    tables = _ret_tables(x.shape[1])
    wrt = w['moe_w_router'].T
    rb = w['moe_router_bias'].reshape(N_EXPERTS, 1)
    for i in range(DEPTH):
        sh1, sc1, g1, sh2, sc2, g2 = [m[:, None, :] for m in jnp.split(mod[i], ADA_CHUNKS, axis=-1)]
        kind, j = i % 3, i // 3
        ng = w['norm_mix_g'][i]
        if kind == 0:
            x = _ret_layer(x, ng, sc1, sh1, g1, tables, w['ret_w_in'][j], w['ret_gn_g'][j], w['ret_gn_b'][j],
                           w['ret_w_out'][j])
        elif kind == 1:
            x = _rwkv_layer(x, ng, sc1, sh1, g1, w['rwkv_mu'][j], w['rwkv_w_r'][j], w['rwkv_w_k'][j],
                            w['rwkv_w_v'][j], w['rwkv_w_o'][j], w['rwkv_decay_w0'][j], w['rwkv_decay_w1'][j],
                            w['rwkv_decay_w2'][j], w['rwkv_iclr_a0'][j], w['rwkv_iclr_a1'][j], w['rwkv_iclr_a2'][j],
                            w['rwkv_gate_g1'][j], w['rwkv_gate_g2'][j], w['rwkv_k_k'][j], w['rwkv_k_a'][j],
                            w['rwkv_r_k'][j], w['rwkv_gn_g'][j], w['rwkv_gn_b'][j])
        else:
            x = _ssd_layer(x, ng, sc1, sh1, g1, w['ssm_w_in'][j], w['ssm_conv_w'][j], w['ssm_conv_b'][j],
                           w['ssm_dt_bias'][j], w['ssm_a_log'][j], w['ssm_d'][j], w['ssm_norm_g'][j],
                           w['ssm_w_out'][j])
        x = _moe(x, w['norm_ffn_g'][i], sc2, sh2, g2, wrt, rb, w['moe_w_gate'][i].astype(BF16),
                 w['moe_w_up'][i].astype(BF16), w['moe_w_down'][i].astype(BF16))
    return _final_norm(x, w['final_norm_g'])


def kernel(x_prompt, x_sample, c_prompt, c_sample,
           ada_w, ada_b, norm_mix_g, norm_ffn_g, final_norm_g,
           ret_w_in, ret_gn_g, ret_gn_b, ret_w_out,
           rwkv_mu, rwkv_w_r, rwkv_w_k, rwkv_w_v, rwkv_w_o,
           rwkv_decay_w0, rwkv_decay_w1, rwkv_decay_w2,
           rwkv_iclr_a0, rwkv_iclr_a1, rwkv_iclr_a2,
           rwkv_gate_g1, rwkv_gate_g2, rwkv_k_k, rwkv_k_a, rwkv_r_k, rwkv_gn_g, rwkv_gn_b,
           ssm_w_in, ssm_conv_w, ssm_conv_b, ssm_dt_bias, ssm_a_log, ssm_d, ssm_norm_g, ssm_w_out,
           moe_w_router, moe_router_bias, moe_w_gate, moe_w_up, moe_w_down):
    w = dict(
        norm_mix_g=norm_mix_g, norm_ffn_g=norm_ffn_g, final_norm_g=final_norm_g,
        ret_w_in=ret_w_in, ret_gn_g=ret_gn_g, ret_gn_b=ret_gn_b, ret_w_out=ret_w_out,
        rwkv_mu=rwkv_mu, rwkv_w_r=rwkv_w_r, rwkv_w_k=rwkv_w_k, rwkv_w_v=rwkv_w_v, rwkv_w_o=rwkv_w_o,
        rwkv_decay_w0=rwkv_decay_w0, rwkv_decay_w1=rwkv_decay_w1, rwkv_decay_w2=rwkv_decay_w2,
        rwkv_iclr_a0=rwkv_iclr_a0, rwkv_iclr_a1=rwkv_iclr_a1, rwkv_iclr_a2=rwkv_iclr_a2,
        rwkv_gate_g1=rwkv_gate_g1, rwkv_gate_g2=rwkv_gate_g2, rwkv_k_k=rwkv_k_k, rwkv_k_a=rwkv_k_a,
        rwkv_r_k=rwkv_r_k, rwkv_gn_g=rwkv_gn_g, rwkv_gn_b=rwkv_gn_b,
        ssm_w_in=ssm_w_in, ssm_conv_w=ssm_conv_w, ssm_conv_b=ssm_conv_b, ssm_dt_bias=ssm_dt_bias,
        ssm_a_log=ssm_a_log, ssm_d=ssm_d, ssm_norm_g=ssm_norm_g, ssm_w_out=ssm_w_out,
        moe_w_router=moe_w_router, moe_router_bias=moe_router_bias, moe_w_gate=moe_w_gate,
        moe_w_up=moe_w_up, moe_w_down=moe_w_down,
    )
    nb = x_prompt.shape[0]
    mod = _ada_mod(jnp.concatenate([c_prompt, c_sample], axis=0), ada_w, ada_b)
    y_prompt = _run_trunk(x_prompt, mod[:, :nb], w)
    y_sample = _run_trunk(x_sample, mod[:, nb:], w)
    return (y_prompt, y_sample)
```

```python
import functools
import math

import jax
import jax.numpy as jnp
from jax import lax
from jax.experimental import pallas as pl
from jax.experimental.pallas import tpu as pltpu

F32 = jnp.float32
BF16 = jnp.bfloat16
HIGHEST = lax.Precision.HIGHEST

V7X_LANES = 128
V7X_VMEM_BYTES = 64 * 1024 * 1024
VMEM_LIMIT = V7X_VMEM_BYTES - 8 * 1024 * 1024

D = 1024
DEPTH = 4
NORM_EPS = 1e-6
ADA_CHUNKS = 6

RET_HEADS = 4
RET_DK = 256
RET_DV = 512
RET_QK = RET_HEADS * RET_DK
RET_V = RET_HEADS * RET_DV
RET_IN = 2 * RET_QK + 2 * RET_V
RET_ROPE_BASE = 10000.0
RET_GN_EPS = 1e-5

RWKV_HEAD = 64
RWKV_HEADS = 16
RWKV_PAIRS = RWKV_HEADS // 2
RWKV_GN_EPS = 64e-5
RWKV_CHUNK = 64

SSM_INNER = 2048
SSM_HEADDIM = 64
SSM_HEADS = 32
SSM_PAIRS = SSM_HEADS // 2
SSM_GROUPS = 4
SSM_STATE = 128
SSM_CONV = 5
SSM_CONV_DIM = SSM_INNER + 2 * SSM_GROUPS * SSM_STATE
SSM_CHUNK = 128

N_EXPERTS = 16
N_GROUPS = 4
EXPERT_FF = 512


def _cparams(*sem):
    return pltpu.CompilerParams(dimension_semantics=sem, vmem_limit_bytes=VMEM_LIMIT)


def _sigmoid(x):
    return 1.0 / (1.0 + jnp.exp(-x))


def _silu(x):
    return x * _sigmoid(x)


def _softplus(x):
    return jnp.maximum(x, 0.0) + jnp.log(1.0 + jnp.exp(-jnp.abs(x)))


def _norm_mod(x, g, sc, sh):
    ms = jnp.mean(x * x, axis=-1, keepdims=True)
    return (x * lax.rsqrt(ms + NORM_EPS) * g) * (1.0 + sc) + sh


def _dot(a, b):
    return jnp.dot(a, b, preferred_element_type=F32)


def _dot_nt(a, b):
    return lax.dot_general(a, b, (((1,), (1,)), ((), ())), preferred_element_type=F32)


def _dot_tn(a, b):
    return lax.dot_general(a, b, (((0,), (0,)), ((), ())), preferred_element_type=F32)


def _split3(x):
    hi = x.astype(BF16)
    r1 = x - hi.astype(F32)
    mid = r1.astype(BF16)
    lo = (r1 - mid.astype(F32)).astype(BF16)
    return hi, mid, lo


def _dot_exact_lhs(a01, x):
    hi, mid, lo = _split3(x)
    return _dot(a01, hi) + _dot(a01, mid) + _dot(a01, lo)


def _dot_exact_rhs(x, b01):
    hi, mid, lo = _split3(x)
    return _dot(hi, b01) + _dot(mid, b01) + _dot(lo, b01)


def _mod_kernel(c_ref, w_ref, b_ref, o_ref):
    s = _silu(c_ref[...])
    o_ref[0] = jnp.dot(s, w_ref[0], preferred_element_type=F32, precision=HIGHEST) + b_ref[0]


def _ada_mod(c, ada_w, ada_b):
    nb = c.shape[0]
    depth, _, f = ada_w.shape
    tn = 1536
    return pl.pallas_call(
        _mod_kernel,
        out_shape=jax.ShapeDtypeStruct((depth, nb, f), F32),
        grid=(depth, f // tn),
        in_specs=[
            pl.BlockSpec((nb, D), lambda l, j: (0, 0)),
            pl.BlockSpec((1, D, tn), lambda l, j: (l, 0, j)),
            pl.BlockSpec((1, 1, tn), lambda l, j: (l, 0, j)),
        ],
        out_specs=pl.BlockSpec((1, nb, tn), lambda l, j: (l, 0, j)),
        compiler_params=_cparams("parallel", "parallel"),
        name="ada_mod",
    )(c, ada_w, ada_b.reshape(depth, 1, f))


def _proj_kernel(tn, x_ref, g_ref, sc_ref, sh_ref, w_ref, o_ref):
    hn = _norm_mod(x_ref[0], g_ref[...], sc_ref[0], sh_ref[0]).astype(BF16)
    for j in range(w_ref.shape[1] // tn):
        cols = slice(j * tn, (j + 1) * tn)
        o_ref[0, :, cols] = _dot(hn, w_ref[:, cols]).astype(o_ref.dtype)


def _norm_proj(x, g, sc, sh, w, out_dtype, tm=512, tn=1024):
    b, t, _ = x.shape
    f = w.shape[1]
    tm = min(tm, t)
    tn = min(tn, f)
    return pl.pallas_call(
        functools.partial(_proj_kernel, tn),
        out_shape=jax.ShapeDtypeStruct((b, t, f), out_dtype),
        grid=(b, t // tm),
        in_specs=[
            pl.BlockSpec((1, tm, D), lambda bi, i: (bi, i, 0)),
            pl.BlockSpec((1, D), lambda bi, i: (0, 0)),
            pl.BlockSpec((1, 1, D), lambda bi, i: (bi, 0, 0)),
            pl.BlockSpec((1, 1, D), lambda bi, i: (bi, 0, 0)),
            pl.BlockSpec((D, f), lambda bi, i: (0, 0)),
        ],
        out_specs=pl.BlockSpec((1, tm, f), lambda bi, i: (bi, i, 0)),
        compiler_params=_cparams("parallel", "parallel"),
        name="norm_proj",
    )(x, g.reshape(1, D), sc, sh, w)


def _normmod_kernel(x_ref, g_ref, sc_ref, sh_ref, o_ref):
    o_ref[0] = _norm_mod(x_ref[0], g_ref[...], sc_ref[0], sh_ref[0])


def _norm_only(x, g, sc, sh, tm=512):
    b, t, _ = x.shape
    tm = min(tm, t)
    return pl.pallas_call(
        _normmod_kernel,
        out_shape=jax.ShapeDtypeStruct((b, t, D), F32),
        grid=(b, t // tm),
        in_specs=[
            pl.BlockSpec((1, tm, D), lambda bi, i: (bi, i, 0)),
            pl.BlockSpec((1, D), lambda bi, i: (0, 0)),
            pl.BlockSpec((1, 1, D), lambda bi, i: (bi, 0, 0)),
            pl.BlockSpec((1, 1, D), lambda bi, i: (bi, 0, 0)),
        ],
        out_specs=pl.BlockSpec((1, tm, D), lambda bi, i: (bi, i, 0)),
        compiler_params=_cparams("parallel", "parallel"),
        name="norm_mod",
    )(x, g.reshape(1, D), sc, sh)


def _out_kernel(y_ref, w_ref, x_ref, g_ref, o_ref):
    o_ref[0] = x_ref[0] + g_ref[0] * _dot(y_ref[0], w_ref[...])


def _out_proj(y, w, x, gate, tm=512):
    b, t, k = y.shape
    tm = min(tm, t)
    return pl.pallas_call(
        _out_kernel,
        out_shape=jax.ShapeDtypeStruct((b, t, D), F32),
        grid=(b, t // tm),
        in_specs=[
            pl.BlockSpec((1, tm, k), lambda bi, i: (bi, i, 0)),
            pl.BlockSpec((k, D), lambda bi, i: (0, 0)),
            pl.BlockSpec((1, tm, D), lambda bi, i: (bi, i, 0)),
            pl.BlockSpec((1, 1, D), lambda bi, i: (bi, 0, 0)),
        ],
        out_specs=pl.BlockSpec((1, tm, D), lambda bi, i: (bi, i, 0)),
        compiler_params=_cparams("parallel", "parallel"),
        name="out_proj",
    )(y, w, x, gate)


def _final_kernel(x_ref, g_ref, o_ref):
    x = x_ref[0]
    ms = jnp.mean(x * x, axis=-1, keepdims=True)
    o_ref[0] = x * lax.rsqrt(ms + NORM_EPS) * g_ref[...]


def _final_norm(x, g, tm=512):
    b, t, _ = x.shape
    tm = min(tm, t)
    return pl.pallas_call(
        _final_kernel,
        out_shape=jax.ShapeDtypeStruct((b, t, D), F32),
        grid=(b, t // tm),
        in_specs=[
            pl.BlockSpec((1, tm, D), lambda bi, i: (bi, i, 0)),
            pl.BlockSpec((1, D), lambda bi, i: (0, 0)),
        ],
        out_specs=pl.BlockSpec((1, tm, D), lambda bi, i: (bi, i, 0)),
        compiler_params=_cparams("parallel", "parallel"),
        name="final_norm",
    )(x, g.reshape(1, D))


def _route(sel, sc):
    n_in = N_EXPERTS // N_GROUPS
    gscore = []
    for g in range(N_GROUPS):
        a, b, c, d = sel[n_in * g:n_in * g + n_in]
        m1, n1 = jnp.maximum(a, b), jnp.minimum(a, b)
        m2, n2 = jnp.maximum(c, d), jnp.minimum(c, d)
        gscore.append(jnp.maximum(m1, m2) + jnp.maximum(jnp.minimum(m1, m2), jnp.maximum(n1, n2)))
    in_group = []
    taken = None
    for g in range(N_GROUPS):
        best = None
        for o in range(g + 1, N_GROUPS):
            c = gscore[g] >= gscore[o]
            best = c if best is None else (best & c)
        if best is None:
            best = jnp.ones_like(gscore[g], dtype=jnp.bool_)
        if taken is not None:
            best = best & jnp.logical_not(taken)
        in_group.append(best)
        taken = best if taken is None else (taken | best)
    vs, ts = [], []
    for k in range(n_in):
        v = sel[3 * n_in + k]
        t = sc[3 * n_in + k]
        for g in range(N_GROUPS - 2, -1, -1):
            v = jnp.where(in_group[g], sel[n_in * g + k], v)
            t = jnp.where(in_group[g], sc[n_in * g + k], t)
        vs.append(v)
        ts.append(t)
    chosen = []
    for i in range(n_in):
        beaten = None
        for j in range(n_in):
            if j == i:
                continue
            c = (vs[j] > vs[i]) if j > i else (vs[j] >= vs[i])
            c = c.astype(F32)
            beaten = c if beaten is None else beaten + c
        chosen.append(beaten < 1.5)
    denom = None
    for k in range(n_in):
        term = jnp.where(chosen[k], ts[k], 0.0)
        denom = term if denom is None else denom + term
    gates = []
    for g in range(N_GROUPS):
        for k in range(n_in):
            gates.append(jnp.where(in_group[g] & chosen[k], ts[k] / denom, 0.0))
    return gates, in_group


MOE_SUB = 256


def _moe_route_kernel(x_ref, g_ref, sc_ref, sh_ref, wrt_ref, rb_ref,
                      xs_ref, gts_ref, pos_ref, flg_ref, gates_ref, grp_ref):
    tm = x_ref.shape[1]
    hn = _norm_mod(x_ref[0], g_ref[...], sc_ref[0], sh_ref[0])
    hn_hi = hn.astype(BF16)
    hn_mid = (hn - hn_hi.astype(F32)).astype(BF16)
    w_hi, w_mid, _ = _split3(wrt_ref[...])
    logits = _dot_nt(w_hi, hn_hi) + (_dot_nt(w_hi, hn_mid) + _dot_nt(w_mid, hn_hi))
    score = _sigmoid(logits)
    sel = score + rb_ref[...]
    gates, in_group = _route([sel[i:i + 1, :] for i in range(N_EXPERTS)],
                             [score[i:i + 1, :] for i in range(N_EXPERTS)])
    gates_ref[...] = jnp.zeros_like(gates_ref)
    for i in range(N_EXPERTS):
        gates_ref[i:i + 1, :] = gates[i]
    grp_ref[...] = jnp.zeros_like(grp_ref)
    for gi in range(N_GROUPS):
        grp_ref[gi:gi + 1, :] = jnp.where(in_group[gi], 1.0, 0.0)
    row = lax.broadcasted_iota(jnp.int32, (tm, tm), 0)
    colm = lax.broadcasted_iota(jnp.int32, (tm, tm), 1)
    tri = jnp.where(row <= colm, 1.0, 0.0).astype(BF16)
    cnt = _dot(grp_ref[...].astype(BF16), tri)
    sizes = [cnt[gi:gi + 1, tm - 1:tm] for gi in range(N_GROUPS)]
    offs = [jnp.zeros((1, 1), F32)]
    for gi in range(N_GROUPS - 1):
        offs.append(offs[-1] + sizes[gi])
    pos = jnp.zeros((1, tm), F32)
    for gi in range(N_GROUPS):
        pos = pos + jnp.where(in_group[gi], offs[gi] + cnt[gi:gi + 1, :] - 1.0, 0.0)
    perm = jnp.where(row.astype(F32) == pos, 1.0, 0.0).astype(BF16)
    xs_ref[0] = _dot(perm, hn_hi).astype(xs_ref.dtype)
    hi, mid, _ = _split3(gates_ref[...])
    gts_ref[0] = _dot_nt(perm, hi) + _dot_nt(perm, mid)
    pos_ref[0] = pos.astype(jnp.int32)
    lane = lax.broadcasted_iota(jnp.int32, (1, V7X_LANES), 1)
    flags = jnp.zeros((1, V7X_LANES), jnp.int32)
    nsub = tm // MOE_SUB
    for gi in range(N_GROUPS):
        for j in range(nsub):
            hit = (offs[gi] < (j + 1.0) * MOE_SUB) & (offs[gi] + sizes[gi] > j * float(MOE_SUB))
            flags = flags + jnp.where((lane == gi * nsub + j) & hit, 1, 0)
    flg_ref[0] = flags


def _moe_expert_kernel(flg_ref, xs_ref, gts_ref, pos_ref, x_ref, g2_ref, wg_ref, wu_ref, wd_ref, o_ref, acc_ref):
    i, e = pl.program_id(1), pl.program_id(2)
    tile = pl.program_id(0) * pl.num_programs(1) + i
    tm = xs_ref.shape[1]
    nsub = tm // MOE_SUB

    @pl.when(e == 0)
    def _():
        acc_ref[...] = jnp.zeros_like(acc_ref)

    grp = e // (N_EXPERTS // N_GROUPS)
    lane = lax.broadcasted_iota(jnp.int32, (MOE_SUB, V7X_LANES), 1)
    for j in range(nsub):
        @pl.when(flg_ref[tile, grp * nsub + j] != 0)
        def _():
            rows = slice(j * MOE_SUB, (j + 1) * MOE_SUB)
            xb = xs_ref[0, rows, :]
            col = jnp.sum(jnp.where(lane == e, gts_ref[0, rows, :], 0.0), axis=1, keepdims=True)
            hid = _silu(_dot(xb, wg_ref[0])) * _dot(xb, wu_ref[0]) * col
            acc_ref[rows, :] += _dot(hid.astype(BF16), wd_ref[0])

    @pl.when(e == N_EXPERTS - 1)
    def _():
        row = lax.broadcasted_iota(jnp.int32, (tm, tm), 0)
        perm = jnp.where(row == pos_ref[0], 1.0, 0.0).astype(BF16)
        o_ref[0] = x_ref[0] + g2_ref[0] * _dot_tn(perm, acc_ref[...].astype(BF16))


def _moe(x, g, sc, sh, g2, wrt, rb, wg, wu, wd, tm=1024):
    b, t, _ = x.shape
    tm = min(tm, t)
    nt = t // tm
    nflag = N_GROUPS * (tm // MOE_SUB)
    tok = lambda bi, i: (bi, i, 0)
    per_b = lambda bi, i: (bi, 0, 0)
    fixed = lambda bi, i: (0, 0)
    xs, gts, pos, flg = pl.pallas_call(
        _moe_route_kernel,
        out_shape=[
            jax.ShapeDtypeStruct((b, t, D), BF16),
            jax.ShapeDtypeStruct((b, t, V7X_LANES), F32),
            jax.ShapeDtypeStruct((b * nt, 1, tm), jnp.int32),
            jax.ShapeDtypeStruct((b * nt, 1, V7X_LANES), jnp.int32),
        ],
        grid=(b, nt),
        in_specs=[
            pl.BlockSpec((1, tm, D), tok),
            pl.BlockSpec((1, D), fixed),
            pl.BlockSpec((1, 1, D), per_b),
            pl.BlockSpec((1, 1, D), per_b),
            pl.BlockSpec((N_EXPERTS, D), fixed),
            pl.BlockSpec((N_EXPERTS, 1), fixed),
        ],
        out_specs=[
            pl.BlockSpec((1, tm, D), tok),
            pl.BlockSpec((1, tm, V7X_LANES), tok),
            pl.BlockSpec((1, 1, tm), lambda bi, i: (bi * nt + i, 0, 0)),
            pl.BlockSpec((1, 1, V7X_LANES), lambda bi, i: (bi * nt + i, 0, 0)),
        ],
        scratch_shapes=[pltpu.VMEM((V7X_LANES, tm), F32), pltpu.VMEM((8, tm), F32)],
        compiler_params=_cparams("parallel", "parallel"),
        name="moe_route",
    )(x, g.reshape(1, D), sc, sh, wrt, rb)
    flags = flg.reshape(b * nt, V7X_LANES)[:, :nflag]
    tok3 = lambda bi, i, e, f: (bi, i, 0)
    return pl.pallas_call(
        _moe_expert_kernel,
        out_shape=jax.ShapeDtypeStruct((b, t, D), F32),
        grid_spec=pltpu.PrefetchScalarGridSpec(
            num_scalar_prefetch=1,
            grid=(b, nt, N_EXPERTS),
            in_specs=[
                pl.BlockSpec((1, tm, D), tok3),
                pl.BlockSpec((1, tm, V7X_LANES), tok3),
                pl.BlockSpec((1, 1, tm), lambda bi, i, e, f: (bi * nt + i, 0, 0)),
                pl.BlockSpec((1, tm, D), tok3),
                pl.BlockSpec((1, 1, D), lambda bi, i, e, f: (bi, 0, 0)),
                pl.BlockSpec((1, D, EXPERT_FF), lambda bi, i, e, f: (e, 0, 0)),
                pl.BlockSpec((1, D, EXPERT_FF), lambda bi, i, e, f: (e, 0, 0)),
                pl.BlockSpec((1, EXPERT_FF, D), lambda bi, i, e, f: (e, 0, 0)),
            ],
            out_specs=pl.BlockSpec((1, tm, D), tok3),
            scratch_shapes=[pltpu.VMEM((tm, D), F32)],
        ),
        compiler_params=_cparams("parallel", "parallel", "arbitrary"),
        name="moe_experts",
    )(flags, xs, gts, pos, x, g2, wg, wu, wd)


def _rotary(x, cos, sin):
    half = x.shape[-1] // 2
    x1, x2 = x[:, :half], x[:, half:]
    return jnp.concatenate([x1 * cos - x2 * sin, x1 * sin + x2 * cos], axis=-1)


def _ret_bwd_kernel(k_ref, v_ref, cos_ref, sin_ref, lg_ref, hb_ref, h_ref):
    c = pl.program_id(2)
    ch = k_ref.shape[1]

    @pl.when(c == 0)
    def _():
        h_ref[...] = jnp.zeros_like(h_ref)

    hb_ref[0, 0, 0] = h_ref[...]
    lg = lg_ref[0][:, :1]
    pos = lax.broadcasted_iota(jnp.int32, (ch, 1), 0).astype(F32)
    kr = _rotary(k_ref[0].astype(F32), cos_ref[...], sin_ref[...]) * (RET_DK ** -0.5)
    kw = (kr * jnp.exp(lg * (pos + 1.0))).astype(BF16)
    h_ref[...] = jnp.exp(lg * ch) * h_ref[...] + _dot_tn(kw, v_ref[0])


def _ret_fwd_kernel(q_ref, k_ref, v_ref, gate_ref, cos_ref, sin_ref, lg_ref, hb_ref, gg_ref, gb_ref,
                    o_ref, h_ref, dec_ref):
    c = pl.program_id(2)
    ch = q_ref.shape[1]
    lg = lg_ref[0][:, :1]

    @pl.when(c == 0)
    def _():
        h_ref[...] = jnp.zeros_like(h_ref)
        ii = lax.broadcasted_iota(jnp.int32, (ch, ch), 0)
        jj = lax.broadcasted_iota(jnp.int32, (ch, ch), 1)
        dec_ref[...] = jnp.exp(lg * jnp.abs(ii - jj).astype(F32))

    cos, sin = cos_ref[...], sin_ref[...]
    pos = lax.broadcasted_iota(jnp.int32, (ch, 1), 0).astype(F32)
    qr = _rotary(q_ref[0].astype(F32), cos, sin)
    kr = _rotary(k_ref[0].astype(F32), cos, sin) * (RET_DK ** -0.5)
    v = v_ref[0]
    s = _dot_nt(qr.astype(BF16), kr.astype(BF16)) * dec_ref[...]
    y = _dot(s.astype(BF16), v)
    q2 = jnp.concatenate([qr * jnp.exp(lg * (pos + 1.0)), qr * jnp.exp(lg * (ch - 1.0 - pos))], axis=-1)
    h2 = jnp.concatenate([h_ref[...], hb_ref[0, 0, 0]], axis=0)
    y = y + _dot(q2.astype(BF16), h2.astype(BF16))
    kw = (kr * jnp.exp(lg * (ch - 1.0 - pos))).astype(BF16)
    h_ref[...] = jnp.exp(lg * ch) * h_ref[...] + _dot_tn(kw, v)
    mu = jnp.mean(y, axis=-1, keepdims=True)
    var = jnp.mean(jnp.square(y - mu), axis=-1, keepdims=True)
    yn = (y - mu) * lax.rsqrt(var + RET_GN_EPS) * gg_ref[...] + gb_ref[...]
    o_ref[0] = (_silu(gate_ref[0].astype(F32)) * yn).astype(o_ref.dtype)


def _retention(proj, cos, sin, lg, gn_g, gn_b, ch=512):
    b, t, _ = proj.shape
    ch = min(ch, t)
    nc = t // ch
    kq = RET_QK // RET_DK
    vq = 2 * RET_QK // RET_DV
    gq = vq + RET_HEADS
    hb = pl.pallas_call(
        _ret_bwd_kernel,
        out_shape=jax.ShapeDtypeStruct((b, RET_HEADS, nc, RET_DK, RET_DV), F32),
        grid=(b, RET_HEADS, nc),
        in_specs=[
            pl.BlockSpec((1, ch, RET_DK), lambda bi, h, c: (bi, nc - 1 - c, kq + h)),
            pl.BlockSpec((1, ch, RET_DV), lambda bi, h, c: (bi, nc - 1 - c, vq + h)),
            pl.BlockSpec((ch, RET_DK // 2), lambda bi, h, c: (nc - 1 - c, 0)),
            pl.BlockSpec((ch, RET_DK // 2), lambda bi, h, c: (nc - 1 - c, 0)),
            pl.BlockSpec((1, 1, V7X_LANES), lambda bi, h, c: (h, 0, 0)),
        ],
        out_specs=pl.BlockSpec((1, 1, 1, RET_DK, RET_DV), lambda bi, h, c: (bi, h, nc - 1 - c, 0, 0)),
        scratch_shapes=[pltpu.VMEM((RET_DK, RET_DV), F32)],
        compiler_params=_cparams("parallel", "parallel", "arbitrary"),
        name="ret_bwd_state",
    )(proj, proj, cos, sin, lg)
    return pl.pallas_call(
        _ret_fwd_kernel,
        out_shape=jax.ShapeDtypeStruct((b, t, RET_V), BF16),
        grid=(b, RET_HEADS, nc),
        in_specs=[
            pl.BlockSpec((1, ch, RET_DK), lambda bi, h, c: (bi, c, h)),
            pl.BlockSpec((1, ch, RET_DK), lambda bi, h, c: (bi, c, kq + h)),
            pl.BlockSpec((1, ch, RET_DV), lambda bi, h, c: (bi, c, vq + h)),
            pl.BlockSpec((1, ch, RET_DV), lambda bi, h, c: (bi, c, gq + h)),
            pl.BlockSpec((ch, RET_DK // 2), lambda bi, h, c: (c, 0)),
            pl.BlockSpec((ch, RET_DK // 2), lambda bi, h, c: (c, 0)),
            pl.BlockSpec((1, 1, V7X_LANES), lambda bi, h, c: (h, 0, 0)),
            pl.BlockSpec((1, 1, 1, RET_DK, RET_DV), lambda bi, h, c: (bi, h, c, 0, 0)),
            pl.BlockSpec((1, RET_DV), lambda bi, h, c: (0, h)),
            pl.BlockSpec((1, RET_DV), lambda bi, h, c: (0, h)),
        ],
        out_specs=pl.BlockSpec((1, ch, RET_DV), lambda bi, h, c: (bi, c, h)),
        scratch_shapes=[pltpu.VMEM((RET_DK, RET_DV), F32), pltpu.VMEM((ch, ch), F32)],
        compiler_params=_cparams("parallel", "parallel", "arbitrary"),
        name="ret_fwd",
    )(proj, proj, proj, proj, cos, sin, lg, hb, gn_g.reshape(1, RET_V), gn_b.reshape(1, RET_V))


def _ret_tables(t):
    half = RET_DK // 2
    inv = RET_ROPE_BASE ** (-jnp.arange(half, dtype=F32) / half)
    ang = jnp.arange(t, dtype=F32)[:, None] * inv[None, :]
    log_gamma = jnp.log(1.0 - 2.0 ** (-5.0 - jnp.arange(RET_HEADS, dtype=F32)))
    lg = jnp.broadcast_to(log_gamma[:, None, None], (RET_HEADS, 1, V7X_LANES))
    return jnp.cos(ang), jnp.sin(ang), lg


def _ret_layer(x, g, sc, sh, gate, tables, w_in, gn_g, gn_b, w_out):
    proj = _norm_proj(x, g, sc, sh, w_in.astype(BF16), BF16)
    y = _retention(proj, *tables, gn_g, gn_b)
    return _out_proj(y, w_out.astype(BF16), x, gate)


CONV_HALO = 16


def _conv_kernel(cur_ref, prev_ref, next_ref, w_ref, b_ref, o_ref):
    i = pl.program_id(1)
    tt = cur_ref.shape[1]
    cur = cur_ref[0].astype(F32)
    prev = jnp.where(i == 0, 0.0, prev_ref[0].astype(F32))
    nxt = jnp.where(i == pl.num_programs(1) - 1, 0.0, next_ref[0].astype(F32))
    ext = jnp.concatenate([prev, cur, nxt], axis=0)
    w = w_ref[...]
    acc = jnp.zeros_like(cur) + b_ref[...]
    for s in range(SSM_CONV):
        off = CONV_HALO + s - SSM_CONV // 2
        acc = acc + w[s:s + 1, :] * ext[off:off + tt]
    o_ref[0] = _silu(acc).astype(o_ref.dtype)


def _ssd_conv(zx, conv_w, conv_b, tt=512, tc=1024):
    b, t, _ = zx.shape
    tt = min(tt, t)
    off = SSM_INNER // tc
    nh = tt // CONV_HALO
    last = t // CONV_HALO - 1
    w = jnp.zeros((8, SSM_CONV_DIM), F32).at[:SSM_CONV].set(conv_w)
    return pl.pallas_call(
        _conv_kernel,
        out_shape=jax.ShapeDtypeStruct((b, t, SSM_CONV_DIM), BF16),
        grid=(b, t // tt, SSM_CONV_DIM // tc),
        in_specs=[
            pl.BlockSpec((1, tt, tc), lambda bi, i, j: (bi, i, off + j)),
            pl.BlockSpec((1, CONV_HALO, tc), lambda bi, i, j: (bi, jnp.maximum(i * nh - 1, 0), off + j)),
            pl.BlockSpec((1, CONV_HALO, tc), lambda bi, i, j: (bi, jnp.minimum((i + 1) * nh, last), off + j)),
            pl.BlockSpec((8, tc), lambda bi, i, j: (0, j)),
            pl.BlockSpec((1, tc), lambda bi, i, j: (0, j)),
        ],
        out_specs=pl.BlockSpec((1, tt, tc), lambda bi, i, j: (bi, i, j)),
        compiler_params=_cparams("parallel", "parallel", "parallel"),
        name="ssd_conv",
    )(zx, zx, zx, w, conv_b.reshape(1, SSM_CONV_DIM))


def _ssd_decays(dt_ref, bias_ref, alog_ref):
    ch = dt_ref.shape[1]
    dt = _softplus(dt_ref[0] + bias_ref[...])
    la = dt * (-jnp.exp(alog_ref[...]))
    ii = lax.broadcasted_iota(jnp.int32, (ch, ch), 0)
    jj = lax.broadcasted_iota(jnp.int32, (ch, ch), 1)
    tri = jnp.where(ii >= jj, 1.0, 0.0).astype(BF16)
    cum = _dot_exact_lhs(tri, la)
    return dt, la, cum


def _pair_cols(lo, arr, h0):
    return jnp.where(lo, arr[:, h0:h0 + 1], arr[:, h0 + 1:h0 + 2])


def _ssd_bwd_kernel(xs_ref, b_ref, dt_ref, bias_ref, alog_ref, hb_ref, h_ref):
    c = pl.program_id(1)
    ch = xs_ref.shape[1]

    @pl.when(c == 0)
    def _():
        h_ref[...] = jnp.zeros_like(h_ref)

    hb_ref[0, 0] = h_ref[...]
    dt, la, cum = _ssd_decays(dt_ref, bias_ref, alog_ref)
    wb = jnp.exp(cum - la) * dt
    etot = jnp.exp(cum[ch - 1:ch, :])
    lo = lax.broadcasted_iota(jnp.int32, (ch, V7X_LANES), 1) < SSM_HEADDIM
    for p in range(SSM_PAIRS):
        hb0 = SSM_HEADS + 2 * p
        g = p // (SSM_PAIRS // SSM_GROUPS)
        sl = slice(p * V7X_LANES, (p + 1) * V7X_LANES)
        xw = (xs_ref[0, :, sl].astype(F32) * _pair_cols(lo, wb, hb0)).astype(BF16)
        bg = b_ref[0, :, g * SSM_STATE:(g + 1) * SSM_STATE]
        h_ref[p] = _pair_cols(lo[:1], etot, hb0) * h_ref[p] + _dot_tn(bg, xw)


def _ssd_fwd_kernel(z_ref, xs_ref, b_ref, c_ref, dt_ref, bias_ref, alog_ref, hb_ref, dsk_ref, ng_ref,
                    o_ref, h_ref, y_ref):
    c = pl.program_id(1)
    ch = xs_ref.shape[1]

    @pl.when(c == 0)
    def _():
        h_ref[...] = jnp.zeros_like(h_ref)

    dt, la, cum = _ssd_decays(dt_ref, bias_ref, alog_ref)
    cbx = cum - la
    tot = cum[ch - 1:ch, :]
    cum_t, cbx_t, dt_t = cum.T, cbx.T, dt.T
    ef = jnp.exp(cum)
    eb = jnp.exp(tot - cbx)
    wf = jnp.exp(tot - cum) * dt
    etot = jnp.exp(tot)
    lo = lax.broadcasted_iota(jnp.int32, (ch, V7X_LANES), 1) < SSM_HEADDIM
    lower = (lax.broadcasted_iota(jnp.int32, (ch, ch), 0) >= lax.broadcasted_iota(jnp.int32, (ch, ch), 1))
    ppg = SSM_PAIRS // SSM_GROUPS
    for g in range(SSM_GROUPS):
        gs = slice(g * SSM_STATE, (g + 1) * SSM_STATE)
        cg = c_ref[0, :, gs]
        bg = b_ref[0, :, gs]
        cb = _dot_nt(cg, bg)
        for q in range(ppg):
            p = g * ppg + q
            sl = slice(p * V7X_LANES, (p + 1) * V7X_LANES)
            x = xs_ref[0, :, sl]
            xf = x.astype(F32)
            y = xf * dsk_ref[:, sl]
            for hh in range(2):
                hf = 2 * p + hh
                hbw = SSM_HEADS + hf
                arg = jnp.where(lower, cum[:, hf:hf + 1] - cum_t[hf:hf + 1, :],
                                cbx_t[hbw:hbw + 1, :] - cbx[:, hbw:hbw + 1])
                m = cb * jnp.exp(arg) * jnp.where(lower, dt_t[hf:hf + 1, :], dt_t[hbw:hbw + 1, :])
                xm = jnp.where(lo if hh == 0 else jnp.logical_not(lo), x, jnp.zeros_like(x))
                y = y + _dot(m.astype(BF16), xm)
            hprev = h_ref[p]
            y = y + _pair_cols(lo, ef, 2 * p) * _dot(cg, hprev.astype(BF16))
            y = y + _pair_cols(lo, eb, SSM_HEADS + 2 * p) * _dot(cg, hb_ref[0, 0, p].astype(BF16))
            y_ref[:, sl] = y * _silu(z_ref[0, :, sl].astype(F32))
            xw = (xf * _pair_cols(lo, wf, 2 * p)).astype(BF16)
            h_ref[p] = _pair_cols(lo[:1], etot, 2 * p) * hprev + _dot_tn(bg, xw)
    gw = SSM_INNER // SSM_GROUPS
    for g in range(SSM_GROUPS):
        gs = slice(g * gw, (g + 1) * gw)
        yg = y_ref[:, gs]
        ms = jnp.mean(yg * yg, axis=-1, keepdims=True)
        o_ref[0, :, gs] = (yg * lax.rsqrt(ms + NORM_EPS) * ng_ref[:, gs]).astype(o_ref.dtype)


def _ssd_scan(zx, xbc, dtr, bias, alog, dsk, ng):
    b, t, _ = xbc.shape
    ch = SSM_CHUNK
    nc = t // ch
    gn = SSM_GROUPS * SSM_STATE
    bq = SSM_INNER // gn
    vec = lambda bi, c: (0, 0)
    hb = pl.pallas_call(
        _ssd_bwd_kernel,
        out_shape=jax.ShapeDtypeStruct((b, nc, SSM_PAIRS, SSM_STATE, V7X_LANES), F32),
        grid=(b, nc),
        in_specs=[
            pl.BlockSpec((1, ch, SSM_INNER), lambda bi, c: (bi, nc - 1 - c, 0)),
            pl.BlockSpec((1, ch, gn), lambda bi, c: (bi, nc - 1 - c, bq)),
            pl.BlockSpec((1, ch, V7X_LANES), lambda bi, c: (bi, nc - 1 - c, 0)),
            pl.BlockSpec((1, V7X_LANES), vec),
            pl.BlockSpec((1, V7X_LANES), vec),
        ],
        out_specs=pl.BlockSpec((1, 1, SSM_PAIRS, SSM_STATE, V7X_LANES), lambda bi, c: (bi, nc - 1 - c, 0, 0, 0)),
        scratch_shapes=[pltpu.VMEM((SSM_PAIRS, SSM_STATE, V7X_LANES), F32)],
        compiler_params=_cparams("parallel", "arbitrary"),
        name="ssd_bwd_state",
    )(xbc, xbc, dtr, bias, alog)
    return pl.pallas_call(
        _ssd_fwd_kernel,
        out_shape=jax.ShapeDtypeStruct((b, t, SSM_INNER), BF16),
        grid=(b, nc),
        in_specs=[
            pl.BlockSpec((1, ch, SSM_INNER), lambda bi, c: (bi, c, 0)),
            pl.BlockSpec((1, ch, SSM_INNER), lambda bi, c: (bi, c, 0)),
            pl.BlockSpec((1, ch, gn), lambda bi, c: (bi, c, bq)),
            pl.BlockSpec((1, ch, gn), lambda bi, c: (bi, c, bq + 1)),
            pl.BlockSpec((1, ch, V7X_LANES), lambda bi, c: (bi, c, 0)),
            pl.BlockSpec((1, V7X_LANES), vec),
            pl.BlockSpec((1, V7X_LANES), vec),
            pl.BlockSpec((1, 1, SSM_PAIRS, SSM_STATE, V7X_LANES), lambda bi, c: (bi, c, 0, 0, 0)),
            pl.BlockSpec((1, SSM_INNER), vec),
            pl.BlockSpec((1, SSM_INNER), vec),
        ],
        out_specs=pl.BlockSpec((1, ch, SSM_INNER), lambda bi, c: (bi, c, 0)),
        scratch_shapes=[pltpu.VMEM((SSM_PAIRS, SSM_STATE, V7X_LANES), F32), pltpu.VMEM((ch, SSM_INNER), F32)],
        compiler_params=_cparams("parallel", "arbitrary"),
        name="ssd_fwd",
    )(zx, xbc, xbc, xbc, dtr, bias, alog, hb, dsk, ng)


def _pad_lanes(v):
    v = v.reshape(1, -1)
    return jnp.pad(v, ((0, 0), (0, V7X_LANES - v.shape[1])))


def _ssd_layer(x, g, sc, sh, gate, w_in, conv_w, conv_b, dt_bias, a_log, d_skip, norm_g, w_out):
    n_main = SSM_INNER + SSM_CONV_DIM
    w_main = w_in[:, :n_main].astype(BF16)
    w_dt = jnp.pad(w_in[:, n_main:], ((0, 0), (0, V7X_LANES - 2 * SSM_HEADS))).astype(BF16)
    zx = _norm_proj(x, g, sc, sh, w_main, BF16)
    dtr = _norm_proj(x, g, sc, sh, w_dt, F32)
    xbc = _ssd_conv(zx, conv_w, conv_b)
    dsk = jnp.repeat(d_skip, SSM_HEADDIM).reshape(1, SSM_INNER)
    y = _ssd_scan(zx, xbc, dtr, _pad_lanes(dt_bias), _pad_lanes(a_log), dsk, norm_g.reshape(1, SSM_INNER))
    return _out_proj(y, w_out.astype(BF16), x, gate)


def _head_sums(x, ones_bd, passes=3):
    parts = _split3(x)[:passes]
    acc = _dot(parts[0], ones_bd)
    for part in parts[1:]:
        acc = acc + _dot(part, ones_bd)
    return acc


def _rwkv_prep_kernel(h_ref, hp_ref, hx_ref, mu_ref, wr_ref, wk_ref, wv_ref, dw1_ref, dw2_ref, a1_ref, a2_ref,
                      g1_ref, g2_ref, w0_ref, a0_ref, kk_ref, ka_ref, ones_ref,
                      r_o, v_o, kn_o, g_o, lwf_o, lwb_o, kdf_o, kdb_o, bf_o, bb_o):
    i = pl.program_id(1)
    h = h_ref[0]
    tt = h.shape[0]
    prev_row = jnp.where(i == 0, 0.0, hp_ref[0][7:8, :])
    next_row = jnp.where(i == pl.num_programs(1) - 1, 0.0, hx_ref[0][0:1, :])
    row = lax.broadcasted_iota(jnp.int32, (tt, 1), 0)
    up = jnp.where(row == 0, prev_row, pltpu.roll(h, 1, axis=0))
    dn = jnp.where(row == tt - 1, next_row, pltpu.roll(h, tt - 1, axis=0))
    xx = 0.5 * (up + dn) - h
    mu = mu_ref[...]
    xr, xw, xk, xv, xa, xg = [(h + xx * mu[j:j + 1, :]).astype(BF16) for j in range(6)]
    r = _dot(xr, wr_ref[...])
    k = _dot(xk, wk_ref[...])
    v = _dot(xv, wv_ref[...])
    g = _dot(_sigmoid(_dot(xg, g1_ref[...])).astype(BF16), g2_ref[...])
    wlo = _dot(jnp.tanh(_dot(xw, dw1_ref[...])).astype(BF16), dw2_ref[...])
    alo = _dot(_dot(xa, a1_ref[...]).astype(BF16), a2_ref[...])
    kkf = k * kk_ref[...]
    ones_bd = ones_ref[...]
    kn_parts = []
    for p in range(RWKV_PAIRS):
        sl = slice(p * V7X_LANES, (p + 1) * V7X_LANES)
        kp = kkf[:, sl]
        kn_parts.append(kp * lax.rsqrt(_head_sums(kp * kp, ones_bd, 2) + 1e-12))
    kn = jnp.concatenate(kn_parts, axis=-1)
    r_o[0] = r.astype(r_o.dtype)
    v_o[0] = v.astype(v_o.dtype)
    kn_o[0] = kn.astype(kn_o.dtype)
    g_o[0] = g.astype(g_o.dtype)
    ka = ka_ref[...]
    for d, (lw_o, kd_o, b_o) in enumerate(((lwf_o, kdf_o, bf_o), (lwb_o, kdb_o, bb_o))):
        w_raw = w0_ref[d:d + 1, :] + wlo[:, d * D:(d + 1) * D]
        lw_o[0] = -jnp.exp(-_softplus(-w_raw) - 0.5)
        a = _sigmoid(a0_ref[d:d + 1, :] + alo[:, d * D:(d + 1) * D])
        kd_o[0] = (k * (1.0 + (a - 1.0) * ka)).astype(kd_o.dtype)
        b_o[0] = (kn * a).astype(b_o.dtype)


def _block_diag2(w):
    z = jnp.zeros_like(w[0])
    return jnp.concatenate([jnp.concatenate([w[0], z], axis=1), jnp.concatenate([z, w[1]], axis=1)], axis=0)


def _rwkv_prep(hn, mu, w_r, w_k, w_v, dw0, dw1, dw2, a0, a1, a2, g1, g2, k_k, k_a, ones_bd, tt=256):
    b, t, _ = hn.shape
    tt = min(tt, t)
    nh = tt // 8
    last = t // 8 - 1
    cat2 = lambda w: jnp.concatenate([w[0], w[1]], axis=1)
    mu8 = jnp.zeros((8, D), F32).at[:mu.shape[0]].set(mu)
    ins = [hn, hn, hn, mu8, w_r.astype(BF16), w_k.astype(BF16), w_v.astype(BF16),
           cat2(dw1).astype(BF16), _block_diag2(dw2).astype(BF16), cat2(a1).astype(BF16), _block_diag2(a2).astype(BF16),
           g1.astype(BF16), g2.astype(BF16), dw0, a0, k_k.reshape(1, D), k_a.reshape(1, D), ones_bd]
    tok = lambda bi, i: (bi, i, 0)
    full = lambda a: pl.BlockSpec(a.shape, lambda bi, i: (0,) * a.ndim)
    in_specs = [
        pl.BlockSpec((1, tt, D), tok),
        pl.BlockSpec((1, 8, D), lambda bi, i: (bi, jnp.maximum(i * nh - 1, 0), 0)),
        pl.BlockSpec((1, 8, D), lambda bi, i: (bi, jnp.minimum((i + 1) * nh, last), 0)),
    ] + [full(a) for a in ins[3:]]
    sds = lambda dt: jax.ShapeDtypeStruct((b, t, D), dt)
    out_dtypes = [BF16, BF16, BF16, BF16, F32, F32, BF16, BF16, BF16, BF16]
    return pl.pallas_call(
        _rwkv_prep_kernel,
        out_shape=[sds(dt) for dt in out_dtypes],
        grid=(b, t // tt),
        in_specs=in_specs,
        out_specs=[pl.BlockSpec((1, tt, D), tok) for _ in out_dtypes],
        compiler_params=_cparams("parallel", "parallel"),
        name="rwkv_prep",
    )(*ins)


def _unit_tri_inverse(a_list):
    n = a_list[0].shape[0]
    eye = jnp.where(lax.broadcasted_iota(jnp.int32, (n, n), 0) == lax.broadcasted_iota(jnp.int32, (n, n), 1), 1.0, 0.0)
    mm = lambda u, w: _dot(u.astype(BF16), w.astype(BF16))
    xs = [eye - a for a in a_list]
    pws = [mm(a, a) for a in a_list]
    k = 2
    while k < RWKV_CHUNK:
        xs = [x + mm(x, pw) for x, pw in zip(xs, pws)]
        k *= 2
        if k < RWKV_CHUNK:
            pws = [mm(pw, pw) for pw in pws]
    return xs


RWKV_PAIR_GROUP = 8


def _rwkv_scan_kernel(reverse, final, r_ref, kn_ref, v_ref, kd_ref, b_ref, lw_ref, *rest):
    if final:
        yb_ref, g_ref, rk_ref, gg_ref, gb_ref, ones_ref, o_ref, s_ref = rest
    else:
        o_ref, s_ref = rest
    c = pl.program_id(1)
    ch = RWKV_CHUNK

    @pl.when(c == 0)
    def _():
        s_ref[...] = jnp.zeros_like(s_ref)

    lw = lw_ref[0]
    ii = lax.broadcasted_iota(jnp.int32, (ch, ch), 0)
    jj = lax.broadcasted_iota(jnp.int32, (ch, ch), 1)
    tri = jnp.where((ii <= jj) if reverse else (ii >= jj), 1.0, 0.0).astype(BF16)
    cum = _dot_exact_lhs(tri, lw)
    cum_ex = cum - lw
    tot = cum[0:1, :] if reverse else cum[ch - 1:ch, :]
    r = r_ref[0].astype(F32)
    kn = kn_ref[0].astype(F32)
    kd = kd_ref[0].astype(F32)
    bb = b_ref[0].astype(F32)
    e_ex = jnp.exp(cum_ex)
    e_neg = jnp.exp(-cum)
    e_end = jnp.exp(tot - cum)
    kq = kn * e_ex
    rt = r * (e_ex if reverse else jnp.exp(cum))
    kdt = kd * e_neg
    bt = bb * e_neg
    kdl = kd * e_end
    bl = bb * e_end
    etot = jnp.exp(tot)

    n2 = 2 * ch
    i2 = lax.broadcasted_iota(jnp.int32, (n2, n2), 0)
    j2 = lax.broadcasted_iota(jnp.int32, (n2, n2), 1)
    same = (i2 < ch) == (j2 < ch)
    ti, tj = i2 & (ch - 1), j2 & (ch - 1)
    strict = same & ((ti < tj) if reverse else (ti > tj))
    incl = strict if reverse else (same & (ti >= tj))
    lo = lax.broadcasted_iota(jnp.int32, (ch, V7X_LANES), 1) < RWKV_HEAD

    v = v_ref[0].astype(F32)

    def stk(a, sl):
        ap = a[:, sl]
        return jnp.concatenate([jnp.where(lo, ap, 0.0), jnp.where(lo, 0.0, ap)], axis=0).astype(BF16)

    for p0 in range(0, RWKV_PAIRS, RWKV_PAIR_GROUP):
        ps = list(range(p0, p0 + RWKV_PAIR_GROUP))
        sls = [slice(p * V7X_LANES, (p + 1) * V7X_LANES) for p in ps]
        kqs = [stk(kq, sl) for sl in sls]
        rts = [stk(rt, sl) for sl in sls]
        kdts = [stk(kdt, sl) for sl in sls]
        bts = [stk(bt, sl) for sl in sls]
        quads = [_dot_nt(jnp.concatenate([x, y], axis=0), jnp.concatenate([kd, bq], axis=0))
                 for x, y, kd, bq in zip(kqs, rts, kdts, bts)]
        a2 = [jnp.where(strict, q[:n2, n2:], 0.0) for q in quads]
        tinv = _unit_tri_inverse(a2)
        vs = [stk(v, sl) for sl in sls]
        a1 = [jnp.where(strict, q[:n2, :n2], 0.0).astype(BF16) for q in quads]
        ss = [s_ref[p] for p in ps]
        sbs = [s.astype(BF16) for s in ss]
        rhs = [_dot_nt(x, sb) + _dot(a, vv) for x, sb, a, vv in zip(kqs, sbs, a1, vs)]
        zs = [_dot(t.astype(BF16), q.astype(BF16)) for t, q in zip(tinv, rhs)]
        zbs = [z.astype(BF16) for z in zs]
        nzbs = [(-z).astype(BF16) for z in zs]
        a34 = [jnp.concatenate([jnp.where(incl, q[n2:, :n2], 0.0), jnp.where(incl, -q[n2:, n2:], 0.0)], axis=1).astype(BF16)
               for q in quads]
        yss = [_dot_nt(x, sb) + _dot(a, jnp.concatenate([vv, zb], axis=0))
               for x, sb, a, vv, zb in zip(rts, sbs, a34, vs, zbs)]
        kdls = [stk(kdl, sl) for sl in sls]
        bls = [stk(bl, sl) for sl in sls]
        for p, sl, s, vv, kk, nzb, bq in zip(ps, sls, ss, vs, kdls, nzbs, bls):
            s_ref[p] = s * etot[:, sl] + _dot_tn(jnp.concatenate([vv, nzb], axis=0), jnp.concatenate([kk, bq], axis=0))
        for sl, ys in zip(sls, yss):
            y = ys[:ch] + ys[ch:]
            if not final:
                o_ref[0, :, sl] = y
            else:
                ones_bd = ones_ref[...]
                ysum = y + yb_ref[0, :, sl]
                mean = _head_sums(ysum, ones_bd, 2) * (1.0 / RWKV_HEAD)
                dev = ysum - mean
                var = _head_sums(dev * dev, ones_bd, 1) * (1.0 / RWKV_HEAD)
                yn = dev * lax.rsqrt(var + RWKV_GN_EPS) * gg_ref[:, sl] + gb_ref[:, sl]
                bonus = _head_sums(r[:, sl] * kd[:, sl] * rk_ref[:, sl], ones_bd, 1) * v[:, sl]
                o_ref[0, :, sl] = ((yn + bonus) * g_ref[0, :, sl].astype(F32)).astype(o_ref.dtype)


def _rwkv_scan(reverse, r, kn, v, kd, bb, lw, extra=None):
    b, t, _ = r.shape
    ch = RWKV_CHUNK
    nc = t // ch
    tok = (lambda bi, c: (bi, nc - 1 - c, 0)) if reverse else (lambda bi, c: (bi, c, 0))
    vec = lambda bi, c: (0, 0)
    blk = pl.BlockSpec((1, ch, D), tok)
    ins = [r, kn, v, kd, bb, lw]
    in_specs = [blk] * 6
    final = extra is not None
    if final:
        yb, g, rk, gg, gb, ones_bd = extra
        ins += [yb, g, rk, gg, gb, ones_bd]
        in_specs += [blk, blk, pl.BlockSpec((1, D), vec), pl.BlockSpec((1, D), vec), pl.BlockSpec((1, D), vec),
                     pl.BlockSpec((V7X_LANES, V7X_LANES), vec)]
    return pl.pallas_call(
        functools.partial(_rwkv_scan_kernel, reverse, final),
        out_shape=jax.ShapeDtypeStruct((b, t, D), BF16 if final else F32),
        grid=(b, nc),
        in_specs=in_specs,
        out_specs=blk,
        scratch_shapes=[pltpu.VMEM((RWKV_PAIRS, V7X_LANES, V7X_LANES), F32)],
        compiler_params=_cparams("parallel", "arbitrary"),
        name="rwkv_scan_bwd" if reverse else "rwkv_scan_fwd",
    )(*ins)


def _rwkv_layer(x, g, sc, sh, gate, mu, w_r, w_k, w_v, w_o, dw0, dw1, dw2, a0, a1, a2, g1, g2,
                k_k, k_a, r_k, gn_g, gn_b):
    lane = jnp.arange(V7X_LANES)
    ones_bd = ((lane[:, None] < RWKV_HEAD) == (lane[None, :] < RWKV_HEAD)).astype(BF16)
    hn = _norm_only(x, g, sc, sh)
    r, v, kn, gg, lwf, lwb, kdf, kdb, bf, bb = _rwkv_prep(hn, mu, w_r, w_k, w_v, dw0, dw1, dw2, a0, a1, a2, g1, g2,
                                                          k_k, k_a, ones_bd)
    yb = _rwkv_scan(True, r, kn, v, kdb, bb, lwb)
    y = _rwkv_scan(False, r, kn, v, kdf, bf, lwf,
                   extra=(yb, gg, r_k.reshape(1, D), gn_g.reshape(1, D), gn_b.reshape(1, D), ones_bd))
    return _out_proj(y, w_o.astype(BF16), x, gate)


def _run_trunk(x, mod, w):
    tables = _ret_tables(x.shape[1])
    wrt = w['moe_w_router'].T
    rb = w['moe_router_bias'].reshape(N_EXPERTS, 1)
    for i in range(DEPTH):
        sh1, sc1, g1, sh2, sc2, g2 = [m[:, None, :] for m in jnp.split(mod[i], ADA_CHUNKS, axis=-1)]
        kind, j = i % 3, i // 3
        ng = w['norm_mix_g'][i]
        if kind == 0:
            x = _ret_layer(x, ng, sc1, sh1, g1, tables, w['ret_w_in'][j], w['ret_gn_g'][j], w['ret_gn_b'][j],
                           w['ret_w_out'][j])
        elif kind == 1:
            x = _rwkv_layer(x, ng, sc1, sh1, g1, w['rwkv_mu'][j], w['rwkv_w_r'][j], w['rwkv_w_k'][j],
                            w['rwkv_w_v'][j], w['rwkv_w_o'][j], w['rwkv_decay_w0'][j], w['rwkv_decay_w1'][j],
                            w['rwkv_decay_w2'][j], w['rwkv_iclr_a0'][j], w['rwkv_iclr_a1'][j], w['rwkv_iclr_a2'][j],
                            w['rwkv_gate_g1'][j], w['rwkv_gate_g2'][j], w['rwkv_k_k'][j], w['rwkv_k_a'][j],
                            w['rwkv_r_k'][j], w['rwkv_gn_g'][j], w['rwkv_gn_b'][j])
        else:
            x = _ssd_layer(x, ng, sc1, sh1, g1, w['ssm_w_in'][j], w['ssm_conv_w'][j], w['ssm_conv_b'][j],
                           w['ssm_dt_bias'][j], w['ssm_a_log'][j], w['ssm_d'][j], w['ssm_norm_g'][j],
                           w['ssm_w_out'][j])
        x = _moe(x, w['norm_ffn_g'][i], sc2, sh2, g2, wrt, rb, w['moe_w_gate'][i].astype(BF16),
                 w['moe_w_up'][i].astype(BF16), w['moe_w_down'][i].astype(BF16))
    return _final_norm(x, w['final_norm_g'])


def kernel(x_prompt, x_sample, c_prompt, c_sample,
           ada_w, ada_b, norm_mix_g, norm_ffn_g, final_norm_g,
           ret_w_in, ret_gn_g, ret_gn_b, ret_w_out,
           rwkv_mu, rwkv_w_r, rwkv_w_k, rwkv_w_v, rwkv_w_o,
           rwkv_decay_w0, rwkv_decay_w1, rwkv_decay_w2,
           rwkv_iclr_a0, rwkv_iclr_a1, rwkv_iclr_a2,
           rwkv_gate_g1, rwkv_gate_g2, rwkv_k_k, rwkv_k_a, rwkv_r_k, rwkv_gn_g, rwkv_gn_b,
           ssm_w_in, ssm_conv_w, ssm_conv_b, ssm_dt_bias, ssm_a_log, ssm_d, ssm_norm_g, ssm_w_out,
           moe_w_router, moe_router_bias, moe_w_gate, moe_w_up, moe_w_down):
    w = dict(
        norm_mix_g=norm_mix_g, norm_ffn_g=norm_ffn_g, final_norm_g=final_norm_g,
        ret_w_in=ret_w_in, ret_gn_g=ret_gn_g, ret_gn_b=ret_gn_b, ret_w_out=ret_w_out,
        rwkv_mu=rwkv_mu, rwkv_w_r=rwkv_w_r, rwkv_w_k=rwkv_w_k, rwkv_w_v=rwkv_w_v, rwkv_w_o=rwkv_w_o,
        rwkv_decay_w0=rwkv_decay_w0, rwkv_decay_w1=rwkv_decay_w1, rwkv_decay_w2=rwkv_decay_w2,
        rwkv_iclr_a0=rwkv_iclr_a0, rwkv_iclr_a1=rwkv_iclr_a1, rwkv_iclr_a2=rwkv_iclr_a2,
        rwkv_gate_g1=rwkv_gate_g1, rwkv_gate_g2=rwkv_gate_g2, rwkv_k_k=rwkv_k_k, rwkv_k_a=rwkv_k_a,
        rwkv_r_k=rwkv_r_k, rwkv_gn_g=rwkv_gn_g, rwkv_gn_b=rwkv_gn_b,
        ssm_w_in=ssm_w_in, ssm_conv_w=ssm_conv_w, ssm_conv_b=ssm_conv_b, ssm_dt_bias=ssm_dt_bias,
        ssm_a_log=ssm_a_log, ssm_d=ssm_d, ssm_norm_g=ssm_norm_g, ssm_w_out=ssm_w_out,
        moe_w_router=moe_w_router, moe_router_bias=moe_router_bias, moe_w_gate=moe_w_gate,
        moe_w_up=moe_w_up, moe_w_down=moe_w_down,
    )
    nb = x_prompt.shape[0]
    mod = _ada_mod(jnp.concatenate([c_prompt, c_sample], axis=0), ada_w, ada_b)
    y_prompt = _run_trunk(x_prompt, mod[:, :nb], w)
    y_sample = _run_trunk(x_sample, mod[:, nb:], w)
    return (y_prompt, y_sample)
```

```python
import functools
import math

import jax
import jax.numpy as jnp
from jax import lax
from jax.experimental import pallas as pl
from jax.experimental.pallas import tpu as pltpu

F32 = jnp.float32
BF16 = jnp.bfloat16
HIGHEST = lax.Precision.HIGHEST

V7X_LANES = 128
V7X_VMEM_BYTES = 64 * 1024 * 1024
VMEM_LIMIT = V7X_VMEM_BYTES - 8 * 1024 * 1024

D = 1024
DEPTH = 4
NORM_EPS = 1e-6
ADA_CHUNKS = 6

RET_HEADS = 4
RET_DK = 256
RET_DV = 512
RET_QK = RET_HEADS * RET_DK
RET_V = RET_HEADS * RET_DV
RET_IN = 2 * RET_QK + 2 * RET_V
RET_ROPE_BASE = 10000.0
RET_GN_EPS = 1e-5

RWKV_HEAD = 64
RWKV_HEADS = 16
RWKV_PAIRS = RWKV_HEADS // 2
RWKV_GN_EPS = 64e-5
RWKV_CHUNK = 64

SSM_INNER = 2048
SSM_HEADDIM = 64
SSM_HEADS = 32
SSM_PAIRS = SSM_HEADS // 2
SSM_GROUPS = 4
SSM_STATE = 128
SSM_CONV = 5
SSM_CONV_DIM = SSM_INNER + 2 * SSM_GROUPS * SSM_STATE
SSM_CHUNK = 128

N_EXPERTS = 16
N_GROUPS = 4
EXPERT_FF = 512


def _cparams(*sem):
    return pltpu.CompilerParams(dimension_semantics=sem, vmem_limit_bytes=VMEM_LIMIT)


def _sigmoid(x):
    return 1.0 / (1.0 + jnp.exp(-x))


def _silu(x):
    return x * _sigmoid(x)


def _softplus(x):
    return jnp.maximum(x, 0.0) + jnp.log(1.0 + jnp.exp(-jnp.abs(x)))


def _norm_mod(x, g, sc, sh):
    ms = jnp.mean(x * x, axis=-1, keepdims=True)
    return (x * lax.rsqrt(ms + NORM_EPS) * g) * (1.0 + sc) + sh


def _dot(a, b):
    return jnp.dot(a, b, preferred_element_type=F32)


def _dot_nt(a, b):
    return lax.dot_general(a, b, (((1,), (1,)), ((), ())), preferred_element_type=F32)


def _dot_tn(a, b):
    return lax.dot_general(a, b, (((0,), (0,)), ((), ())), preferred_element_type=F32)


def _split3(x):
    hi = x.astype(BF16)
    r1 = x - hi.astype(F32)
    mid = r1.astype(BF16)
    lo = (r1 - mid.astype(F32)).astype(BF16)
    return hi, mid, lo


def _dot_exact_lhs(a01, x):
    hi, mid, lo = _split3(x)
    return _dot(a01, hi) + _dot(a01, mid) + _dot(a01, lo)


def _dot_exact_rhs(x, b01):
    hi, mid, lo = _split3(x)
    return _dot(hi, b01) + _dot(mid, b01) + _dot(lo, b01)


def _mod_kernel(c_ref, w_ref, b_ref, o_ref):
    s = _silu(c_ref[...])
    o_ref[0] = jnp.dot(s, w_ref[0], preferred_element_type=F32, precision=HIGHEST) + b_ref[0]


def _ada_mod(c, ada_w, ada_b):
    nb = c.shape[0]
    depth, _, f = ada_w.shape
    tn = 1536
    return pl.pallas_call(
        _mod_kernel,
        out_shape=jax.ShapeDtypeStruct((depth, nb, f), F32),
        grid=(depth, f // tn),
        in_specs=[
            pl.BlockSpec((nb, D), lambda l, j: (0, 0)),
            pl.BlockSpec((1, D, tn), lambda l, j: (l, 0, j)),
            pl.BlockSpec((1, 1, tn), lambda l, j: (l, 0, j)),
        ],
        out_specs=pl.BlockSpec((1, nb, tn), lambda l, j: (l, 0, j)),
        compiler_params=_cparams("parallel", "parallel"),
        name="ada_mod",
    )(c, ada_w, ada_b.reshape(depth, 1, f))


def _proj_kernel(tn, x_ref, g_ref, sc_ref, sh_ref, w_ref, o_ref):
    hn = _norm_mod(x_ref[0], g_ref[...], sc_ref[0], sh_ref[0]).astype(BF16)
    for j in range(w_ref.shape[1] // tn):
        cols = slice(j * tn, (j + 1) * tn)
        o_ref[0, :, cols] = _dot(hn, w_ref[:, cols]).astype(o_ref.dtype)


def _norm_proj(x, g, sc, sh, w, out_dtype, tm=512, tn=1024):
    b, t, _ = x.shape
    f = w.shape[1]
    tm = min(tm, t)
    tn = min(tn, f)
    return pl.pallas_call(
        functools.partial(_proj_kernel, tn),
        out_shape=jax.ShapeDtypeStruct((b, t, f), out_dtype),
        grid=(b, t // tm),
        in_specs=[
            pl.BlockSpec((1, tm, D), lambda bi, i: (bi, i, 0)),
            pl.BlockSpec((1, D), lambda bi, i: (0, 0)),
            pl.BlockSpec((1, 1, D), lambda bi, i: (bi, 0, 0)),
            pl.BlockSpec((1, 1, D), lambda bi, i: (bi, 0, 0)),
            pl.BlockSpec((D, f), lambda bi, i: (0, 0)),
        ],
        out_specs=pl.BlockSpec((1, tm, f), lambda bi, i: (bi, i, 0)),
        compiler_params=_cparams("parallel", "parallel"),
        name="norm_proj",
    )(x, g.reshape(1, D), sc, sh, w)


def _out_kernel(y_ref, w_ref, x_ref, g_ref, o_ref):
    o_ref[0] = x_ref[0] + g_ref[0] * _dot(y_ref[0], w_ref[...])


def _out_proj(y, w, x, gate, tm=512):
    b, t, k = y.shape
    tm = min(tm, t)
    return pl.pallas_call(
        _out_kernel,
        out_shape=jax.ShapeDtypeStruct((b, t, D), F32),
        grid=(b, t // tm),
        in_specs=[
            pl.BlockSpec((1, tm, k), lambda bi, i: (bi, i, 0)),
            pl.BlockSpec((k, D), lambda bi, i: (0, 0)),
            pl.BlockSpec((1, tm, D), lambda bi, i: (bi, i, 0)),
            pl.BlockSpec((1, 1, D), lambda bi, i: (bi, 0, 0)),
        ],
        out_specs=pl.BlockSpec((1, tm, D), lambda bi, i: (bi, i, 0)),
        compiler_params=_cparams("parallel", "parallel"),
        name="out_proj",
    )(y, w, x, gate)


def _route(sel, sc):
    n_in = N_EXPERTS // N_GROUPS
    gscore = []
    for g in range(N_GROUPS):
        a, b, c, d = sel[n_in * g:n_in * g + n_in]
        m1, n1 = jnp.maximum(a, b), jnp.minimum(a, b)
        m2, n2 = jnp.maximum(c, d), jnp.minimum(c, d)
        gscore.append(jnp.maximum(m1, m2) + jnp.maximum(jnp.minimum(m1, m2), jnp.maximum(n1, n2)))
    in_group = []
    taken = None
    for g in range(N_GROUPS):
        best = None
        for o in range(g + 1, N_GROUPS):
            c = gscore[g] >= gscore[o]
            best = c if best is None else (best & c)
        if best is None:
            best = jnp.ones_like(gscore[g], dtype=jnp.bool_)
        if taken is not None:
            best = best & jnp.logical_not(taken)
        in_group.append(best)
        taken = best if taken is None else (taken | best)
    vs, ts = [], []
    for k in range(n_in):
        v = sel[3 * n_in + k]
        t = sc[3 * n_in + k]
        for g in range(N_GROUPS - 2, -1, -1):
            v = jnp.where(in_group[g], sel[n_in * g + k], v)
            t = jnp.where(in_group[g], sc[n_in * g + k], t)
        vs.append(v)
        ts.append(t)
    chosen = []
    for i in range(n_in):
        beaten = None
        for j in range(n_in):
            if j == i:
                continue
            c = (vs[j] > vs[i]) if j > i else (vs[j] >= vs[i])
            c = c.astype(F32)
            beaten = c if beaten is None else beaten + c
        chosen.append(beaten < 1.5)
    denom = None
    for k in range(n_in):
        term = jnp.where(chosen[k], ts[k], 0.0)
        denom = term if denom is None else denom + term
    gates = []
    for g in range(N_GROUPS):
        for k in range(n_in):
            gates.append(jnp.where(in_group[g] & chosen[k], ts[k] / denom, 0.0))
    return gates, in_group


MOE_SUB = 256
MOE_ALIGN = 16
MOE_MAX_CHUNKS = 5


def _moe_route_kernel(x_ref, g_ref, sc_ref, sh_ref, wrt_ref, rb_ref,
                      xs_ref, gts_ref, pos_ref, flg_ref, gates_ref, grp_ref):
    tm = x_ref.shape[1]
    hn = _norm_mod(x_ref[0], g_ref[...], sc_ref[0], sh_ref[0])
    hn_hi = hn.astype(BF16)
    hn_mid = (hn - hn_hi.astype(F32)).astype(BF16)
    w_hi, w_mid, _ = _split3(wrt_ref[...])
    logits = _dot_nt(w_hi, hn_hi) + (_dot_nt(w_hi, hn_mid) + _dot_nt(w_mid, hn_hi))
    score = _sigmoid(logits)
    sel = score + rb_ref[...]
    gates, in_group = _route([sel[i:i + 1, :] for i in range(N_EXPERTS)],
                             [score[i:i + 1, :] for i in range(N_EXPERTS)])
    gates_ref[...] = jnp.zeros_like(gates_ref)
    for i in range(N_EXPERTS):
        gates_ref[i:i + 1, :] = gates[i]
    grp_ref[...] = jnp.zeros_like(grp_ref)
    for gi in range(N_GROUPS):
        grp_ref[gi:gi + 1, :] = jnp.where(in_group[gi], 1.0, 0.0)
    row = lax.broadcasted_iota(jnp.int32, (tm, tm), 0)
    colm = lax.broadcasted_iota(jnp.int32, (tm, tm), 1)
    tri = jnp.where(row <= colm, 1.0, 0.0).astype(BF16)
    cnt = _dot(grp_ref[...].astype(BF16), tri)
    sizes = [cnt[gi:gi + 1, tm - 1:tm] for gi in range(N_GROUPS)]
    offs = [jnp.zeros((1, 1), F32)]
    for gi in range(N_GROUPS - 1):
        offs.append(offs[-1] + sizes[gi])
    pos = jnp.zeros((1, tm), F32)
    for gi in range(N_GROUPS):
        pos = pos + jnp.where(in_group[gi], offs[gi] + cnt[gi:gi + 1, :] - 1.0, 0.0)
    perm = jnp.where(row.astype(F32) == pos, 1.0, 0.0).astype(BF16)
    xs_ref[0, :tm, :] = _dot(perm, hn_hi).astype(xs_ref.dtype)
    xs_ref[0, tm:, :] = jnp.zeros((MOE_SUB, D), xs_ref.dtype)
    hi, mid, _ = _split3(gates_ref[...])
    gts_ref[0, :tm, :] = _dot_nt(perm, hi) + _dot_nt(perm, mid)
    gts_ref[0, tm:, :] = jnp.zeros((MOE_SUB, V7X_LANES), F32)
    pos_ref[0] = pos.astype(jnp.int32)
    lane = lax.broadcasted_iota(jnp.int32, (1, V7X_LANES), 1)
    info = jnp.zeros((1, V7X_LANES), F32)
    for gi in range(N_GROUPS):
        start = jnp.floor(offs[gi] * (1.0 / MOE_ALIGN)) * MOE_ALIGN
        chunks = jnp.floor((offs[gi] + sizes[gi] - start + (MOE_SUB - 1.0)) * (1.0 / MOE_SUB))
        chunks = jnp.where(sizes[gi] > 0.0, chunks, 0.0)
        info = info + jnp.where(lane == gi, start, 0.0) + jnp.where(lane == N_GROUPS + gi, chunks, 0.0)
    flg_ref[0] = info.astype(jnp.int32)


def _moe_expert_kernel(flg_ref, xs_ref, gts_ref, pos_ref, x_ref, g2_ref, wg_ref, wu_ref, wd_ref, *rest):
    fg_ref = rest[0] if len(rest) == 3 else None
    o_ref, acc_ref = rest[-2:]
    i, e = pl.program_id(1), pl.program_id(2)
    tile = pl.program_id(0) * pl.num_programs(1) + i
    tm = x_ref.shape[1]

    @pl.when(e == 0)
    def _():
        acc_ref[...] = jnp.zeros_like(acc_ref)

    grp = e // (N_EXPERTS // N_GROUPS)
    start = flg_ref[tile, grp]
    chunks = flg_ref[tile, N_GROUPS + grp]
    lane = lax.broadcasted_iota(jnp.int32, (MOE_SUB, V7X_LANES), 1)
    for j in range(MOE_MAX_CHUNKS):
        @pl.when(j < chunks)
        def _():
            rows = pl.ds(pl.multiple_of(start + j * MOE_SUB, MOE_ALIGN), MOE_SUB)
            xb = xs_ref[0, rows, :]
            col = jnp.sum(jnp.where(lane == e, gts_ref[0, rows, :], 0.0), axis=1, keepdims=True)
            hid = _silu(_dot(xb, wg_ref[0])) * _dot(xb, wu_ref[0]) * col
            acc_ref[rows, :] += _dot(hid.astype(BF16), wd_ref[0])

    @pl.when(e == N_EXPERTS - 1)
    def _():
        row = lax.broadcasted_iota(jnp.int32, (tm, tm), 0)
        perm = jnp.where(row == pos_ref[0], 1.0, 0.0).astype(BF16)
        y = x_ref[0] + g2_ref[0] * _dot_tn(perm, acc_ref[:tm, :].astype(BF16))
        if fg_ref is not None:
            y = y * lax.rsqrt(jnp.mean(y * y, axis=-1, keepdims=True) + NORM_EPS) * fg_ref[...]
        o_ref[0] = y


def _moe(x, g, sc, sh, g2, wrt, rb, wg, wu, wd, final_g=None, tm=1024):
    b, t, _ = x.shape
    tm = min(tm, t)
    assert (tm + MOE_ALIGN - 1 + MOE_SUB - 1) // MOE_SUB <= MOE_MAX_CHUNKS
    nt = t // tm
    rp = tm + MOE_SUB
    tok = lambda bi, i: (bi, i, 0)
    per_tile = lambda bi, i: (bi * nt + i, 0, 0)
    per_b = lambda bi, i: (bi, 0, 0)
    fixed = lambda bi, i: (0, 0)
    xs, gts, pos, flg = pl.pallas_call(
        _moe_route_kernel,
        out_shape=[
            jax.ShapeDtypeStruct((b * nt, rp, D), BF16),
            jax.ShapeDtypeStruct((b * nt, rp, V7X_LANES), F32),
            jax.ShapeDtypeStruct((b * nt, 1, tm), jnp.int32),
            jax.ShapeDtypeStruct((b * nt, 1, V7X_LANES), jnp.int32),
        ],
        grid=(b, nt),
        in_specs=[
            pl.BlockSpec((1, tm, D), tok),
            pl.BlockSpec((1, D), fixed),
            pl.BlockSpec((1, 1, D), per_b),
            pl.BlockSpec((1, 1, D), per_b),
            pl.BlockSpec((N_EXPERTS, D), fixed),
            pl.BlockSpec((N_EXPERTS, 1), fixed),
        ],
        out_specs=[
            pl.BlockSpec((1, rp, D), per_tile),
            pl.BlockSpec((1, rp, V7X_LANES), per_tile),
            pl.BlockSpec((1, 1, tm), per_tile),
            pl.BlockSpec((1, 1, V7X_LANES), per_tile),
        ],
        scratch_shapes=[pltpu.VMEM((V7X_LANES, tm), F32), pltpu.VMEM((8, tm), F32)],
        compiler_params=_cparams("parallel", "parallel"),
        name="moe_route",
    )(x, g.reshape(1, D), sc, sh, wrt, rb)
    info = flg.reshape(b * nt, V7X_LANES)[:, :2 * N_GROUPS]
    tok3 = lambda bi, i, e, f: (bi, i, 0)
    tile3 = lambda bi, i, e, f: (bi * nt + i, 0, 0)
    return pl.pallas_call(
        _moe_expert_kernel,
        out_shape=jax.ShapeDtypeStruct((b, t, D), F32),
        grid_spec=pltpu.PrefetchScalarGridSpec(
            num_scalar_prefetch=1,
            grid=(b, nt, N_EXPERTS),
            in_specs=[
                pl.BlockSpec((1, rp, D), tile3),
                pl.BlockSpec((1, rp, V7X_LANES), tile3),
                pl.BlockSpec((1, 1, tm), tile3),
                pl.BlockSpec((1, tm, D), tok3),
                pl.BlockSpec((1, 1, D), lambda bi, i, e, f: (bi, 0, 0)),
                pl.BlockSpec((1, D, EXPERT_FF), lambda bi, i, e, f: (e, 0, 0)),
                pl.BlockSpec((1, D, EXPERT_FF), lambda bi, i, e, f: (e, 0, 0)),
                pl.BlockSpec((1, EXPERT_FF, D), lambda bi, i, e, f: (e, 0, 0)),
            ] + ([] if final_g is None else [pl.BlockSpec((1, D), lambda bi, i, e, f: (0, 0))]),
            out_specs=pl.BlockSpec((1, tm, D), tok3),
            scratch_shapes=[pltpu.VMEM((rp, D), F32)],
        ),
        compiler_params=_cparams("parallel", "parallel", "arbitrary"),
        name="moe_experts",
    )(info, xs, gts, pos, x, g2, wg, wu, wd, *([] if final_g is None else [final_g.reshape(1, D)]))


def _rotary(x, cos, sin):
    half = x.shape[-1] // 2
    x1, x2 = x[:, :half], x[:, half:]
    return jnp.concatenate([x1 * cos - x2 * sin, x1 * sin + x2 * cos], axis=-1)


def _ret_bwd_kernel(k_ref, v_ref, cos_ref, sin_ref, lg_ref, hb_ref, h_ref):
    c = pl.program_id(2)
    ch = k_ref.shape[1]

    @pl.when(c == 0)
    def _():
        h_ref[...] = jnp.zeros_like(h_ref)

    hb_ref[0, 0, 0] = h_ref[...]
    lg = lg_ref[0][:, :1]
    pos = lax.broadcasted_iota(jnp.int32, (ch, 1), 0).astype(F32)
    kr = _rotary(k_ref[0].astype(F32), cos_ref[...], sin_ref[...]) * (RET_DK ** -0.5)
    kw = (kr * jnp.exp(lg * (pos + 1.0))).astype(BF16)
    h_ref[...] = jnp.exp(lg * ch) * h_ref[...] + _dot_tn(kw, v_ref[0])


def _ret_fwd_kernel(q_ref, k_ref, v_ref, gate_ref, cos_ref, sin_ref, lg_ref, hb_ref, gg_ref, gb_ref,
                    o_ref, h_ref, dec_ref):
    c = pl.program_id(2)
    ch = q_ref.shape[1]
    lg = lg_ref[0][:, :1]

    @pl.when(c == 0)
    def _():
        h_ref[...] = jnp.zeros_like(h_ref)
        ii = lax.broadcasted_iota(jnp.int32, (ch, ch), 0)
        jj = lax.broadcasted_iota(jnp.int32, (ch, ch), 1)
        dec_ref[...] = jnp.exp(lg * jnp.abs(ii - jj).astype(F32))

    cos, sin = cos_ref[...], sin_ref[...]
    pos = lax.broadcasted_iota(jnp.int32, (ch, 1), 0).astype(F32)
    qr = _rotary(q_ref[0].astype(F32), cos, sin)
    kr = _rotary(k_ref[0].astype(F32), cos, sin) * (RET_DK ** -0.5)
    v = v_ref[0]
    s = _dot_nt(qr.astype(BF16), kr.astype(BF16)) * dec_ref[...]
    y = _dot(s.astype(BF16), v)
    q2 = jnp.concatenate([qr * jnp.exp(lg * (pos + 1.0)), qr * jnp.exp(lg * (ch - 1.0 - pos))], axis=-1)
    h2 = jnp.concatenate([h_ref[...], hb_ref[0, 0, 0]], axis=0)
    y = y + _dot(q2.astype(BF16), h2.astype(BF16))
    kw = (kr * jnp.exp(lg * (ch - 1.0 - pos))).astype(BF16)
    h_ref[...] = jnp.exp(lg * ch) * h_ref[...] + _dot_tn(kw, v)
    mu = jnp.mean(y, axis=-1, keepdims=True)
    var = jnp.mean(jnp.square(y - mu), axis=-1, keepdims=True)
    yn = (y - mu) * lax.rsqrt(var + RET_GN_EPS) * gg_ref[...] + gb_ref[...]
    gate = gate_ref[0]
    o_ref[0] = (_silu(gate) * yn.astype(BF16)).astype(o_ref.dtype)


def _retention(proj, cos, sin, lg, gn_g, gn_b, ch=512):
    b, t, _ = proj.shape
    ch = min(ch, t)
    nc = t // ch
    kq = RET_QK // RET_DK
    vq = 2 * RET_QK // RET_DV
    gq = vq + RET_HEADS
    hb = pl.pallas_call(
        _ret_bwd_kernel,
        out_shape=jax.ShapeDtypeStruct((b, RET_HEADS, nc, RET_DK, RET_DV), F32),
        grid=(b, RET_HEADS, nc),
        in_specs=[
            pl.BlockSpec((1, ch, RET_DK), lambda bi, h, c: (bi, nc - 1 - c, kq + h)),
            pl.BlockSpec((1, ch, RET_DV), lambda bi, h, c: (bi, nc - 1 - c, vq + h)),
            pl.BlockSpec((ch, RET_DK // 2), lambda bi, h, c: (nc - 1 - c, 0)),
            pl.BlockSpec((ch, RET_DK // 2), lambda bi, h, c: (nc - 1 - c, 0)),
            pl.BlockSpec((1, 1, V7X_LANES), lambda bi, h, c: (h, 0, 0)),
        ],
        out_specs=pl.BlockSpec((1, 1, 1, RET_DK, RET_DV), lambda bi, h, c: (bi, h, nc - 1 - c, 0, 0)),
        scratch_shapes=[pltpu.VMEM((RET_DK, RET_DV), F32)],
        compiler_params=_cparams("parallel", "parallel", "arbitrary"),
        name="ret_bwd_state",
    )(proj, proj, cos, sin, lg)
    return pl.pallas_call(
        _ret_fwd_kernel,
        out_shape=jax.ShapeDtypeStruct((b, t, RET_V), BF16),
        grid=(b, RET_HEADS, nc),
        in_specs=[
            pl.BlockSpec((1, ch, RET_DK), lambda bi, h, c: (bi, c, h)),
            pl.BlockSpec((1, ch, RET_DK), lambda bi, h, c: (bi, c, kq + h)),
            pl.BlockSpec((1, ch, RET_DV), lambda bi, h, c: (bi, c, vq + h)),
            pl.BlockSpec((1, ch, RET_DV), lambda bi, h, c: (bi, c, gq + h)),
            pl.BlockSpec((ch, RET_DK // 2), lambda bi, h, c: (c, 0)),
            pl.BlockSpec((ch, RET_DK // 2), lambda bi, h, c: (c, 0)),
            pl.BlockSpec((1, 1, V7X_LANES), lambda bi, h, c: (h, 0, 0)),
            pl.BlockSpec((1, 1, 1, RET_DK, RET_DV), lambda bi, h, c: (bi, h, c, 0, 0)),
            pl.BlockSpec((1, RET_DV), lambda bi, h, c: (0, h)),
            pl.BlockSpec((1, RET_DV), lambda bi, h, c: (0, h)),
        ],
        out_specs=pl.BlockSpec((1, ch, RET_DV), lambda bi, h, c: (bi, c, h)),
        scratch_shapes=[pltpu.VMEM((RET_DK, RET_DV), F32), pltpu.VMEM((ch, ch), F32)],
        compiler_params=_cparams("parallel", "parallel", "arbitrary"),
        name="ret_fwd",
    )(proj, proj, proj, proj, cos, sin, lg, hb, gn_g.reshape(1, RET_V), gn_b.reshape(1, RET_V))


def _ret_tables(t):
    half = RET_DK // 2
    inv = RET_ROPE_BASE ** (-jnp.arange(half, dtype=F32) / half)
    ang = jnp.arange(t, dtype=F32)[:, None] * inv[None, :]
    log_gamma = jnp.log(1.0 - 2.0 ** (-5.0 - jnp.arange(RET_HEADS, dtype=F32)))
    lg = jnp.broadcast_to(log_gamma[:, None, None], (RET_HEADS, 1, V7X_LANES))
    return jnp.cos(ang), jnp.sin(ang), lg


def _ret_layer(x, g, sc, sh, gate, tables, w_in, gn_g, gn_b, w_out):
    proj = _norm_proj(x, g, sc, sh, w_in.astype(BF16), BF16)
    y = _retention(proj, *tables, gn_g, gn_b)
    return _out_proj(y, w_out.astype(BF16), x, gate)


CONV_HALO = 16


def _conv_kernel(cur_ref, prev_ref, next_ref, w_ref, b_ref, o_ref):
    i = pl.program_id(1)
    tt = cur_ref.shape[1]
    cur = cur_ref[0].astype(F32)
    prev = jnp.where(i == 0, 0.0, prev_ref[0].astype(F32))
    nxt = jnp.where(i == pl.num_programs(1) - 1, 0.0, next_ref[0].astype(F32))
    ext = jnp.concatenate([prev, cur, nxt], axis=0)
    w = w_ref[...]
    acc = jnp.zeros_like(cur) + b_ref[...]
    for s in range(SSM_CONV):
        off = CONV_HALO + s - SSM_CONV // 2
        acc = acc + w[s:s + 1, :] * ext[off:off + tt]
    o_ref[0] = _silu(acc).astype(o_ref.dtype)


def _ssd_conv(zx, conv_w, conv_b, tt=512, tc=1024):
    b, t, _ = zx.shape
    tt = min(tt, t)
    off = SSM_INNER // tc
    nh = tt // CONV_HALO
    last = t // CONV_HALO - 1
    w = jnp.zeros((8, SSM_CONV_DIM), F32).at[:SSM_CONV].set(conv_w)
    return pl.pallas_call(
        _conv_kernel,
        out_shape=jax.ShapeDtypeStruct((b, t, SSM_CONV_DIM), BF16),
        grid=(b, t // tt, SSM_CONV_DIM // tc),
        in_specs=[
            pl.BlockSpec((1, tt, tc), lambda bi, i, j: (bi, i, off + j)),
            pl.BlockSpec((1, CONV_HALO, tc), lambda bi, i, j: (bi, jnp.maximum(i * nh - 1, 0), off + j)),
            pl.BlockSpec((1, CONV_HALO, tc), lambda bi, i, j: (bi, jnp.minimum((i + 1) * nh, last), off + j)),
            pl.BlockSpec((8, tc), lambda bi, i, j: (0, j)),
            pl.BlockSpec((1, tc), lambda bi, i, j: (0, j)),
        ],
        out_specs=pl.BlockSpec((1, tt, tc), lambda bi, i, j: (bi, i, j)),
        compiler_params=_cparams("parallel", "parallel", "parallel"),
        name="ssd_conv",
    )(zx, zx, zx, w, conv_b.reshape(1, SSM_CONV_DIM))


def _ssd_decays(dt_ref, bias_ref, alog_ref):
    ch = dt_ref.shape[1]
    dt = _softplus(dt_ref[0] + bias_ref[...])
    la = dt * (-jnp.exp(alog_ref[...]))
    ii = lax.broadcasted_iota(jnp.int32, (ch, ch), 0)
    jj = lax.broadcasted_iota(jnp.int32, (ch, ch), 1)
    tri = jnp.where(ii >= jj, 1.0, 0.0).astype(BF16)
    cum = _dot_exact_lhs(tri, la)
    return dt, la, cum


def _pair_cols(lo, arr, h0):
    return jnp.where(lo, arr[:, h0:h0 + 1], arr[:, h0 + 1:h0 + 2])


def _ssd_bwd_kernel(xs_ref, b_ref, dt_ref, bias_ref, alog_ref, hb_ref, dtv_ref, cum_ref, h_ref):
    c = pl.program_id(1)
    ch = xs_ref.shape[1]

    @pl.when(c == 0)
    def _():
        h_ref[...] = jnp.zeros_like(h_ref)

    hb_ref[0, 0] = h_ref[...]
    dt, la, cum = _ssd_decays(dt_ref, bias_ref, alog_ref)
    dtv_ref[0] = dt
    cum_ref[0] = cum
    wb = jnp.exp(cum - la) * dt
    etot = jnp.exp(cum[ch - 1:ch, :])
    lo = lax.broadcasted_iota(jnp.int32, (ch, V7X_LANES), 1) < SSM_HEADDIM
    for p in range(SSM_PAIRS):
        hb0 = SSM_HEADS + 2 * p
        g = p // (SSM_PAIRS // SSM_GROUPS)
        sl = slice(p * V7X_LANES, (p + 1) * V7X_LANES)
        xw = (xs_ref[0, :, sl].astype(F32) * _pair_cols(lo, wb, hb0)).astype(BF16)
        bg = b_ref[0, :, g * SSM_STATE:(g + 1) * SSM_STATE]
        h_ref[p] = _pair_cols(lo[:1], etot, hb0) * h_ref[p] + _dot_tn(bg, xw)


def _ssd_fwd_kernel(z_ref, xs_ref, b_ref, c_ref, dtv_ref, cum_ref, alog_ref, hb_ref, dsk_ref, ng_ref,
                    o_ref, h_ref, y_ref):
    c = pl.program_id(1)
    ch = xs_ref.shape[1]

    @pl.when(c == 0)
    def _():
        h_ref[...] = jnp.zeros_like(h_ref)

    dt, cum = dtv_ref[0], cum_ref[0]
    cbx = cum - dt * (-jnp.exp(alog_ref[...]))
    tot = cum[ch - 1:ch, :]
    cum_t, cbx_t, dt_t = cum.T, cbx.T, dt.T
    ef = jnp.exp(cum)
    eb = jnp.exp(tot - cbx)
    wf = jnp.exp(tot - cum) * dt
    etot = jnp.exp(tot)
    lo = lax.broadcasted_iota(jnp.int32, (ch, V7X_LANES), 1) < SSM_HEADDIM
    lower = (lax.broadcasted_iota(jnp.int32, (ch, ch), 0) >= lax.broadcasted_iota(jnp.int32, (ch, ch), 1))
    ppg = SSM_PAIRS // SSM_GROUPS
    for g in range(SSM_GROUPS):
        gs = slice(g * SSM_STATE, (g + 1) * SSM_STATE)
        cg = c_ref[0, :, gs]
        bg = b_ref[0, :, gs]
        cb = _dot_nt(cg, bg)
        for q in range(ppg):
            p = g * ppg + q
            sl = slice(p * V7X_LANES, (p + 1) * V7X_LANES)
            x = xs_ref[0, :, sl]
            xf = x.astype(F32)
            y = xf * dsk_ref[:, sl]
            for hh in range(2):
                hf = 2 * p + hh
                hbw = SSM_HEADS + hf
                arg = jnp.where(lower, cum[:, hf:hf + 1] - cum_t[hf:hf + 1, :],
                                cbx_t[hbw:hbw + 1, :] - cbx[:, hbw:hbw + 1])
                m = cb * jnp.exp(arg) * jnp.where(lower, dt_t[hf:hf + 1, :], dt_t[hbw:hbw + 1, :])
                xm = jnp.where(lo if hh == 0 else jnp.logical_not(lo), x, jnp.zeros_like(x))
                y = y + _dot(m.astype(BF16), xm)
            hprev = h_ref[p]
            y = y + _pair_cols(lo, ef, 2 * p) * _dot(cg, hprev.astype(BF16))
            y = y + _pair_cols(lo, eb, SSM_HEADS + 2 * p) * _dot(cg, hb_ref[0, 0, p].astype(BF16))
            y_ref[:, sl] = y * _silu(z_ref[0, :, sl].astype(F32))
            xw = (xf * _pair_cols(lo, wf, 2 * p)).astype(BF16)
            h_ref[p] = _pair_cols(lo[:1], etot, 2 * p) * hprev + _dot_tn(bg, xw)
    gw = SSM_INNER // SSM_GROUPS
    for g in range(SSM_GROUPS):
        gs = slice(g * gw, (g + 1) * gw)
        yg = y_ref[:, gs]
        ms = jnp.mean(yg * yg, axis=-1, keepdims=True)
        o_ref[0, :, gs] = (yg * lax.rsqrt(ms + NORM_EPS) * ng_ref[:, gs]).astype(o_ref.dtype)


def _ssd_scan(zx, xbc, dtr, bias, alog, dsk, ng):
    b, t, _ = xbc.shape
    ch = SSM_CHUNK
    nc = t // ch
    gn = SSM_GROUPS * SSM_STATE
    bq = SSM_INNER // gn
    vec = lambda bi, c: (0, 0)
    lanes_rev = pl.BlockSpec((1, ch, V7X_LANES), lambda bi, c: (bi, nc - 1 - c, 0))
    hb, dtv, cum = pl.pallas_call(
        _ssd_bwd_kernel,
        out_shape=[jax.ShapeDtypeStruct((b, nc, SSM_PAIRS, SSM_STATE, V7X_LANES), F32),
                   jax.ShapeDtypeStruct((b, t, V7X_LANES), F32), jax.ShapeDtypeStruct((b, t, V7X_LANES), F32)],
        grid=(b, nc),
        in_specs=[
            pl.BlockSpec((1, ch, SSM_INNER), lambda bi, c: (bi, nc - 1 - c, 0)),
            pl.BlockSpec((1, ch, gn), lambda bi, c: (bi, nc - 1 - c, bq)),
            pl.BlockSpec((1, ch, V7X_LANES), lambda bi, c: (bi, nc - 1 - c, 0)),
            pl.BlockSpec((1, V7X_LANES), vec),
            pl.BlockSpec((1, V7X_LANES), vec),
        ],
        out_specs=[pl.BlockSpec((1, 1, SSM_PAIRS, SSM_STATE, V7X_LANES), lambda bi, c: (bi, nc - 1 - c, 0, 0, 0)),
                   lanes_rev, lanes_rev],
        scratch_shapes=[pltpu.VMEM((SSM_PAIRS, SSM_STATE, V7X_LANES), F32)],
        compiler_params=_cparams("parallel", "arbitrary"),
        name="ssd_bwd_state",
    )(xbc, xbc, dtr, bias, alog)
    return pl.pallas_call(
        _ssd_fwd_kernel,
        out_shape=jax.ShapeDtypeStruct((b, t, SSM_INNER), BF16),
        grid=(b, nc),
        in_specs=[
            pl.BlockSpec((1, ch, SSM_INNER), lambda bi, c: (bi, c, 0)),
            pl.BlockSpec((1, ch, SSM_INNER), lambda bi, c: (bi, c, 0)),
            pl.BlockSpec((1, ch, gn), lambda bi, c: (bi, c, bq)),
            pl.BlockSpec((1, ch, gn), lambda bi, c: (bi, c, bq + 1)),
            pl.BlockSpec((1, ch, V7X_LANES), lambda bi, c: (bi, c, 0)),
            pl.BlockSpec((1, ch, V7X_LANES), lambda bi, c: (bi, c, 0)),
            pl.BlockSpec((1, V7X_LANES), vec),
            pl.BlockSpec((1, 1, SSM_PAIRS, SSM_STATE, V7X_LANES), lambda bi, c: (bi, c, 0, 0, 0)),
            pl.BlockSpec((1, SSM_INNER), vec),
            pl.BlockSpec((1, SSM_INNER), vec),
        ],
        out_specs=pl.BlockSpec((1, ch, SSM_INNER), lambda bi, c: (bi, c, 0)),
        scratch_shapes=[pltpu.VMEM((SSM_PAIRS, SSM_STATE, V7X_LANES), F32), pltpu.VMEM((ch, SSM_INNER), F32)],
        compiler_params=_cparams("parallel", "arbitrary"),
        name="ssd_fwd",
    )(zx, xbc, xbc, xbc, dtv, cum, alog, hb, dsk, ng)


def _pad_lanes(v):
    v = v.reshape(1, -1)
    return jnp.pad(v, ((0, 0), (0, V7X_LANES - v.shape[1])))


def _ssd_layer(x, g, sc, sh, gate, w_in, conv_w, conv_b, dt_bias, a_log, d_skip, norm_g, w_out):
    n_main = SSM_INNER + SSM_CONV_DIM
    w_main = w_in[:, :n_main].astype(BF16)
    w_dt = jnp.pad(w_in[:, n_main:], ((0, 0), (0, V7X_LANES - 2 * SSM_HEADS))).astype(BF16)
    zx = _norm_proj(x, g, sc, sh, w_main, BF16)
    dtr = _norm_proj(x, g, sc, sh, w_dt, F32)
    xbc = _ssd_conv(zx, conv_w, conv_b)
    dsk = jnp.repeat(d_skip, SSM_HEADDIM).reshape(1, SSM_INNER)
    y = _ssd_scan(zx, xbc, dtr, _pad_lanes(dt_bias), _pad_lanes(a_log), dsk, norm_g.reshape(1, SSM_INNER))
    return _out_proj(y, w_out.astype(BF16), x, gate)


def _head_sums(x, ones_bd, passes=3):
    parts = _split3(x)[:passes]
    acc = _dot(parts[0], ones_bd)
    for part in parts[1:]:
        acc = acc + _dot(part, ones_bd)
    return acc


def _rwkv_prep_kernel(x_ref, xp_ref, xn_ref, ng_ref, sc_ref, sh_ref, mu_ref, wr_ref, wk_ref, wv_ref,
                      dw1_ref, dw2_ref, a1_ref, a2_ref, g1_ref, g2_ref, w0_ref, a0_ref, kk_ref, ka_ref, ones_ref,
                      r_o, v_o, kn_o, g_o, lwf_o, lwb_o, kdf_o, kdb_o, bf_o, bb_o):
    i = pl.program_id(1)
    norm = lambda xr: _norm_mod(xr, ng_ref[...], sc_ref[0], sh_ref[0])
    h = norm(x_ref[0])
    tt = h.shape[0]
    prev_row = jnp.where(i == 0, 0.0, norm(xp_ref[0])[7:8, :])
    next_row = jnp.where(i == pl.num_programs(1) - 1, 0.0, norm(xn_ref[0])[0:1, :])
    row = lax.broadcasted_iota(jnp.int32, (tt, 1), 0)
    up = jnp.where(row == 0, prev_row, pltpu.roll(h, 1, axis=0))
    dn = jnp.where(row == tt - 1, next_row, pltpu.roll(h, tt - 1, axis=0))
    xx = 0.5 * (up + dn) - h
    mu = mu_ref[...]
    xr, xw, xk, xv, xa, xg = [(h + xx * mu[j:j + 1, :]).astype(BF16) for j in range(6)]
    r = _dot(xr, wr_ref[...])
    k = _dot(xk, wk_ref[...])
    v = _dot(xv, wv_ref[...])
    g = _dot(_sigmoid(_dot(xg, g1_ref[...])).astype(BF16), g2_ref[...])
    wlo = _dot(jnp.tanh(_dot(xw, dw1_ref[...])).astype(BF16), dw2_ref[...])
    alo = _dot(_dot(xa, a1_ref[...]).astype(BF16), a2_ref[...])
    kkf = k * kk_ref[...]
    ones_bd = ones_ref[...]
    kn_parts = []
    for p in range(RWKV_PAIRS):
        sl = slice(p * V7X_LANES, (p + 1) * V7X_LANES)
        kp = kkf[:, sl]
        kn_parts.append(kp * lax.rsqrt(_head_sums(kp * kp, ones_bd, 2) + 1e-12))
    kn = jnp.concatenate(kn_parts, axis=-1)
    r_o[0] = r.astype(r_o.dtype)
    v_o[0] = v.astype(v_o.dtype)
    kn_o[0] = kn.astype(kn_o.dtype)
    g_o[0] = g.astype(g_o.dtype)
    ka = ka_ref[...]
    for d, (lw_o, kd_o, b_o) in enumerate(((lwf_o, kdf_o, bf_o), (lwb_o, kdb_o, bb_o))):
        w_raw = w0_ref[d:d + 1, :] + wlo[:, d * D:(d + 1) * D]
        lw_o[0] = -math.exp(-0.5) * _sigmoid(w_raw)
        a = _sigmoid(a0_ref[d:d + 1, :] + alo[:, d * D:(d + 1) * D])
        kd_o[0] = (k * (1.0 + (a - 1.0) * ka)).astype(kd_o.dtype)
        b_o[0] = (kn * a).astype(b_o.dtype)


def _block_diag2(w):
    z = jnp.zeros_like(w[0])
    return jnp.concatenate([jnp.concatenate([w[0], z], axis=1), jnp.concatenate([z, w[1]], axis=1)], axis=0)


def _rwkv_prep(x, ng, sc, sh, mu, w_r, w_k, w_v, dw0, dw1, dw2, a0, a1, a2, g1, g2, k_k, k_a, ones_bd, tt=256):
    b, t, _ = x.shape
    tt = min(tt, t)
    nh = tt // 8
    last = t // 8 - 1
    cat2 = lambda w: jnp.concatenate([w[0], w[1]], axis=1)
    mu8 = jnp.zeros((8, D), F32).at[:mu.shape[0]].set(mu)
    consts = [mu8, w_r.astype(BF16), w_k.astype(BF16), w_v.astype(BF16),
              cat2(dw1).astype(BF16), _block_diag2(dw2).astype(BF16), cat2(a1).astype(BF16), _block_diag2(a2).astype(BF16),
              g1.astype(BF16), g2.astype(BF16), dw0, a0, k_k.reshape(1, D), k_a.reshape(1, D), ones_bd]
    ins = [x, x, x, ng.reshape(1, D), sc, sh] + consts
    tok = lambda bi, i: (bi, i, 0)
    per_b = lambda bi, i: (bi, 0, 0)
    full = lambda a: pl.BlockSpec(a.shape, lambda bi, i: (0,) * a.ndim)
    in_specs = [
        pl.BlockSpec((1, tt, D), tok),
        pl.BlockSpec((1, 8, D), lambda bi, i: (bi, jnp.maximum(i * nh - 1, 0), 0)),
        pl.BlockSpec((1, 8, D), lambda bi, i: (bi, jnp.minimum((i + 1) * nh, last), 0)),
        pl.BlockSpec((1, D), lambda bi, i: (0, 0)),
        pl.BlockSpec((1, 1, D), per_b),
        pl.BlockSpec((1, 1, D), per_b),
    ] + [full(a) for a in consts]
    sds = lambda dt: jax.ShapeDtypeStruct((b, t, D), dt)
    out_dtypes = [BF16, BF16, BF16, BF16, F32, F32, BF16, BF16, BF16, BF16]
    return pl.pallas_call(
        _rwkv_prep_kernel,
        out_shape=[sds(dt) for dt in out_dtypes],
        grid=(b, t // tt),
        in_specs=in_specs,
        out_specs=[pl.BlockSpec((1, tt, D), tok) for _ in out_dtypes],
        compiler_params=_cparams("parallel", "parallel"),
        name="rwkv_prep",
    )(*ins)


def _unit_tri_inverse(a_list):
    n = a_list[0].shape[0]
    eye = jnp.where(lax.broadcasted_iota(jnp.int32, (n, n), 0) == lax.broadcasted_iota(jnp.int32, (n, n), 1), 1.0, 0.0)
    mm = lambda u, w: _dot(u.astype(BF16), w.astype(BF16))
    xs = [eye - a for a in a_list]
    pws = [mm(a, a) for a in a_list]
    k = 2
    while k < RWKV_CHUNK:
        xs = [x + mm(x, pw) for x, pw in zip(xs, pws)]
        k *= 2
        if k < RWKV_CHUNK:
            pws = [mm(pw, pw) for pw in pws]
    return xs


RWKV_STACK_HEADS = 2
RWKV_STACK_W = RWKV_STACK_HEADS * RWKV_HEAD
RWKV_STACKS = RWKV_HEADS // RWKV_STACK_HEADS


def _rwkv_scan_kernel(reverse, final, r_ref, kn_ref, v_ref, kd_ref, b_ref, lw_ref, *rest):
    if final:
        yb_ref, g_ref, rk_ref, gg_ref, gb_ref, ones_ref, o_ref, s_ref = rest
    else:
        o_ref, s_ref = rest
    c = pl.program_id(1)
    ch = RWKV_CHUNK
    nh, w = RWKV_STACK_HEADS, RWKV_STACK_W

    @pl.when(c == 0)
    def _():
        s_ref[...] = jnp.zeros_like(s_ref)

    lw = lw_ref[0]
    ii = lax.broadcasted_iota(jnp.int32, (ch, ch), 0)
    jj = lax.broadcasted_iota(jnp.int32, (ch, ch), 1)
    tri = jnp.where((ii <= jj) if reverse else (ii >= jj), 1.0, 0.0).astype(BF16)
    cum = _dot_exact_lhs(tri, lw)
    cum_ex = cum - lw
    tot = cum[0:1, :] if reverse else cum[ch - 1:ch, :]
    r = r_ref[0].astype(F32)
    kn = kn_ref[0].astype(F32)
    kd = kd_ref[0].astype(F32)
    bb = b_ref[0].astype(F32)
    e_ex = jnp.exp(cum_ex)
    e_neg = jnp.exp(-cum)
    e_end = jnp.exp(tot - cum)
    kq = kn * e_ex
    rt = r * (e_ex if reverse else jnp.exp(cum))
    kdt = kd * e_neg
    bt = bb * e_neg
    kdl = kd * e_end
    bl = bb * e_end
    etot = jnp.exp(tot)

    v = v_ref[0].astype(F32)

    n = nh * ch
    i2 = lax.broadcasted_iota(jnp.int32, (n, n), 0)
    j2 = lax.broadcasted_iota(jnp.int32, (n, n), 1)
    same = (i2 // ch) == (j2 // ch)
    ti, tj = i2 % ch, j2 % ch
    strict = same & ((ti < tj) if reverse else (ti > tj))
    incl = strict if reverse else (same & (ti >= tj))
    lane_head = lax.broadcasted_iota(jnp.int32, (ch, w), 1) // RWKV_HEAD

    def stk(a, sl):
        ap = a[:, sl]
        return jnp.concatenate([jnp.where(lane_head == k, ap, 0.0) for k in range(nh)], axis=0).astype(BF16)

    sls = [slice(p * w, (p + 1) * w) for p in range(RWKV_STACKS)]
    kqs = [stk(kq, sl) for sl in sls]
    rts = [stk(rt, sl) for sl in sls]
    kdts = [stk(kdt, sl) for sl in sls]
    bts = [stk(bt, sl) for sl in sls]
    quads = [_dot_nt(jnp.concatenate([x, y], axis=0), jnp.concatenate([kd_, bq], axis=0))
             for x, y, kd_, bq in zip(kqs, rts, kdts, bts)]
    a2 = [jnp.where(strict, q[:n, n:], 0.0) for q in quads]
    tinv = _unit_tri_inverse(a2)
    vs = [stk(v, sl) for sl in sls]
    a1 = [jnp.where(strict, q[:n, :n], 0.0).astype(BF16) for q in quads]
    ss = [s_ref[p] for p in range(RWKV_STACKS)]
    sbs = [s.astype(BF16) for s in ss]
    rhs = [_dot_nt(x, sb) + _dot(a, vv) for x, sb, a, vv in zip(kqs, sbs, a1, vs)]
    zs = [_dot(t.astype(BF16), q.astype(BF16)) for t, q in zip(tinv, rhs)]
    zbs = [z.astype(BF16) for z in zs]
    nzbs = [(-z).astype(BF16) for z in zs]
    a34 = [jnp.concatenate([jnp.where(incl, q[n:, :n], 0.0), jnp.where(incl, -q[n:, n:], 0.0)], axis=1).astype(BF16)
           for q in quads]
    yss = [_dot_nt(x, sb) + _dot(a, jnp.concatenate([vv, zb], axis=0))
           for x, sb, a, vv, zb in zip(rts, sbs, a34, vs, zbs)]
    kdls = [stk(kdl, sl) for sl in sls]
    bls = [stk(bl, sl) for sl in sls]
    for p, (sl, s, vv, kk, nzb, bq) in enumerate(zip(sls, ss, vs, kdls, nzbs, bls)):
        s_ref[p] = s * etot[:, sl] + _dot_tn(jnp.concatenate([vv, nzb], axis=0), jnp.concatenate([kk, bq], axis=0))
    for sl, ys in zip(sls, yss):
        y = ys[:ch]
        for k in range(1, nh):
            y = y + ys[k * ch:(k + 1) * ch]
        if not final:
            o_ref[0, :, sl] = y
        else:
            ones_bd = ones_ref[...]
            ysum = y + yb_ref[0, :, sl]
            mean = _head_sums(ysum, ones_bd, 2) * (1.0 / RWKV_HEAD)
            dev = ysum - mean
            var = _head_sums(dev * dev, ones_bd, 1) * (1.0 / RWKV_HEAD)
            yn = dev * lax.rsqrt(var + RWKV_GN_EPS) * gg_ref[:, sl] + gb_ref[:, sl]
            bonus = _head_sums(r[:, sl] * kd[:, sl] * rk_ref[:, sl], ones_bd, 1) * v[:, sl]
            o_ref[0, :, sl] = ((yn + bonus) * g_ref[0, :, sl].astype(F32)).astype(o_ref.dtype)


def _rwkv_scan(reverse, r, kn, v, kd, bb, lw, extra=None):
    b, t, _ = r.shape
    ch = RWKV_CHUNK
    nc = t // ch
    tok = (lambda bi, c: (bi, nc - 1 - c, 0)) if reverse else (lambda bi, c: (bi, c, 0))
    vec = lambda bi, c: (0, 0)
    blk = pl.BlockSpec((1, ch, D), tok)
    ins = [r, kn, v, kd, bb, lw]
    in_specs = [blk] * 6
    final = extra is not None
    if final:
        yb, g, rk, gg, gb, ones_bd = extra
        ins += [yb, g, rk, gg, gb, ones_bd]
        in_specs += [blk, blk, pl.BlockSpec((1, D), vec), pl.BlockSpec((1, D), vec), pl.BlockSpec((1, D), vec),
                     pl.BlockSpec((RWKV_STACK_W, RWKV_STACK_W), vec)]
    return pl.pallas_call(
        functools.partial(_rwkv_scan_kernel, reverse, final),
        out_shape=jax.ShapeDtypeStruct((b, t, D), BF16 if final else F32),
        grid=(b, nc),
        in_specs=in_specs,
        out_specs=blk,
        scratch_shapes=[pltpu.VMEM((RWKV_STACKS, RWKV_STACK_W, RWKV_STACK_W), F32)],
        compiler_params=_cparams("parallel", "arbitrary"),
        name="rwkv_scan_bwd" if reverse else "rwkv_scan_fwd",
    )(*ins)


def _rwkv_layer(x, g, sc, sh, gate, mu, w_r, w_k, w_v, w_o, dw0, dw1, dw2, a0, a1, a2, g1, g2,
                k_k, k_a, r_k, gn_g, gn_b):
    lane = jnp.arange(RWKV_STACK_W) // RWKV_HEAD
    ones_w = (lane[:, None] == lane[None, :]).astype(BF16)
    ones_bd = ones_w[:V7X_LANES, :V7X_LANES]
    r, v, kn, gg, lwf, lwb, kdf, kdb, bf, bb = _rwkv_prep(x, g, sc, sh, mu, w_r, w_k, w_v, dw0, dw1, dw2, a0, a1, a2,
                                                          g1, g2, k_k, k_a, ones_bd)
    yb = _rwkv_scan(True, r, kn, v, kdb, bb, lwb)
    y = _rwkv_scan(False, r, kn, v, kdf, bf, lwf,
                   extra=(yb, gg, r_k.reshape(1, D), gn_g.reshape(1, D), gn_b.reshape(1, D), ones_w))
    return _out_proj(y, w_o.astype(BF16), x, gate)


def _run_trunk(x, mod, w):
    tables = _ret_tables(x.shape[1])
    wrt = w['moe_w_router'].T
    rb = w['moe_router_bias'].reshape(N_EXPERTS, 1)
    for i in range(DEPTH):
        sh1, sc1, g1, sh2, sc2, g2 = [m[:, None, :] for m in jnp.split(mod[i], ADA_CHUNKS, axis=-1)]
        kind, j = i % 3, i // 3
        ng = w['norm_mix_g'][i]
        if kind == 0:
            x = _ret_layer(x, ng, sc1, sh1, g1, tables, w['ret_w_in'][j], w['ret_gn_g'][j], w['ret_gn_b'][j],
                           w['ret_w_out'][j])
        elif kind == 1:
            x = _rwkv_layer(x, ng, sc1, sh1, g1, w['rwkv_mu'][j], w['rwkv_w_r'][j], w['rwkv_w_k'][j],
                            w['rwkv_w_v'][j], w['rwkv_w_o'][j], w['rwkv_decay_w0'][j], w['rwkv_decay_w1'][j],
                            w['rwkv_decay_w2'][j], w['rwkv_iclr_a0'][j], w['rwkv_iclr_a1'][j], w['rwkv_iclr_a2'][j],
                            w['rwkv_gate_g1'][j], w['rwkv_gate_g2'][j], w['rwkv_k_k'][j], w['rwkv_k_a'][j],
                            w['rwkv_r_k'][j], w['rwkv_gn_g'][j], w['rwkv_gn_b'][j])
        else:
            x = _ssd_layer(x, ng, sc1, sh1, g1, w['ssm_w_in'][j], w['ssm_conv_w'][j], w['ssm_conv_b'][j],
                           w['ssm_dt_bias'][j], w['ssm_a_log'][j], w['ssm_d'][j], w['ssm_norm_g'][j],
                           w['ssm_w_out'][j])
        x = _moe(x, w['norm_ffn_g'][i], sc2, sh2, g2, wrt, rb, w['moe_w_gate'][i].astype(BF16),
                 w['moe_w_up'][i].astype(BF16), w['moe_w_down'][i].astype(BF16),
                 final_g=w['final_norm_g'] if i == DEPTH - 1 else None)
    return x


def kernel(x_prompt, x_sample, c_prompt, c_sample,
           ada_w, ada_b, norm_mix_g, norm_ffn_g, final_norm_g,
           ret_w_in, ret_gn_g, ret_gn_b, ret_w_out,
           rwkv_mu, rwkv_w_r, rwkv_w_k, rwkv_w_v, rwkv_w_o,
           rwkv_decay_w0, rwkv_decay_w1, rwkv_decay_w2,
           rwkv_iclr_a0, rwkv_iclr_a1, rwkv_iclr_a2,
           rwkv_gate_g1, rwkv_gate_g2, rwkv_k_k, rwkv_k_a, rwkv_r_k, rwkv_gn_g, rwkv_gn_b,
           ssm_w_in, ssm_conv_w, ssm_conv_b, ssm_dt_bias, ssm_a_log, ssm_d, ssm_norm_g, ssm_w_out,
           moe_w_router, moe_router_bias, moe_w_gate, moe_w_up, moe_w_down):
    w = dict(
        norm_mix_g=norm_mix_g, norm_ffn_g=norm_ffn_g, final_norm_g=final_norm_g,
        ret_w_in=ret_w_in, ret_gn_g=ret_gn_g, ret_gn_b=ret_gn_b, ret_w_out=ret_w_out,
        rwkv_mu=rwkv_mu, rwkv_w_r=rwkv_w_r, rwkv_w_k=rwkv_w_k, rwkv_w_v=rwkv_w_v, rwkv_w_o=rwkv_w_o,
        rwkv_decay_w0=rwkv_decay_w0, rwkv_decay_w1=rwkv_decay_w1, rwkv_decay_w2=rwkv_decay_w2,
        rwkv_iclr_a0=rwkv_iclr_a0, rwkv_iclr_a1=rwkv_iclr_a1, rwkv_iclr_a2=rwkv_iclr_a2,
        rwkv_gate_g1=rwkv_gate_g1, rwkv_gate_g2=rwkv_gate_g2, rwkv_k_k=rwkv_k_k, rwkv_k_a=rwkv_k_a,
        rwkv_r_k=rwkv_r_k, rwkv_gn_g=rwkv_gn_g, rwkv_gn_b=rwkv_gn_b,
        ssm_w_in=ssm_w_in, ssm_conv_w=ssm_conv_w, ssm_conv_b=ssm_conv_b, ssm_dt_bias=ssm_dt_bias,
        ssm_a_log=ssm_a_log, ssm_d=ssm_d, ssm_norm_g=ssm_norm_g, ssm_w_out=ssm_w_out,
        moe_w_router=moe_w_router, moe_router_bias=moe_router_bias, moe_w_gate=moe_w_gate,
        moe_w_up=moe_w_up, moe_w_down=moe_w_down,
    )
    nb = x_prompt.shape[0]
    mod = _ada_mod(jnp.concatenate([c_prompt, c_sample], axis=0), ada_w, ada_b)
    y_prompt = _run_trunk(x_prompt, mod[:, :nb], w)
    y_sample = _run_trunk(x_sample, mod[:, nb:], w)
    return (y_prompt, y_sample)
```

```python
import functools
import math

import jax
import jax.numpy as jnp
from jax import lax
from jax.experimental import pallas as pl
from jax.experimental.pallas import tpu as pltpu

F32 = jnp.float32
BF16 = jnp.bfloat16
HIGHEST = lax.Precision.HIGHEST

V7X_LANES = 128
V7X_VMEM_BYTES = 64 * 1024 * 1024
VMEM_LIMIT = V7X_VMEM_BYTES - 8 * 1024 * 1024

D = 1024
DEPTH = 4
NORM_EPS = 1e-6
ADA_CHUNKS = 6

RET_HEADS = 4
RET_DK = 256
RET_DV = 512
RET_QK = RET_HEADS * RET_DK
RET_V = RET_HEADS * RET_DV
RET_IN = 2 * RET_QK + 2 * RET_V
RET_ROPE_BASE = 10000.0
RET_GN_EPS = 1e-5

RWKV_HEAD = 64
RWKV_HEADS = 16
RWKV_PAIRS = RWKV_HEADS // 2
RWKV_GN_EPS = 64e-5
RWKV_CHUNK = 64

SSM_INNER = 2048
SSM_HEADDIM = 64
SSM_HEADS = 32
SSM_PAIRS = SSM_HEADS // 2
SSM_GROUPS = 4
SSM_STATE = 128
SSM_CONV = 5
SSM_CONV_DIM = SSM_INNER + 2 * SSM_GROUPS * SSM_STATE
SSM_CHUNK = 128

N_EXPERTS = 16
N_GROUPS = 4
EXPERT_FF = 512


def _cparams(*sem):
    return pltpu.CompilerParams(dimension_semantics=sem, vmem_limit_bytes=VMEM_LIMIT)


def _sigmoid(x):
    return 1.0 / (1.0 + jnp.exp(-x))


def _silu(x):
    return x * _sigmoid(x)


def _softplus(x):
    return jnp.maximum(x, 0.0) + jnp.log(1.0 + jnp.exp(-jnp.abs(x)))


def _norm_mod(x, g, sc, sh):
    ms = jnp.mean(x * x, axis=-1, keepdims=True)
    return (x * lax.rsqrt(ms + NORM_EPS) * g) * (1.0 + sc) + sh


def _dot(a, b):
    return jnp.dot(a, b, preferred_element_type=F32)


def _dot_nt(a, b):
    return lax.dot_general(a, b, (((1,), (1,)), ((), ())), preferred_element_type=F32)


def _dot_tn(a, b):
    return lax.dot_general(a, b, (((0,), (0,)), ((), ())), preferred_element_type=F32)


def _split3(x):
    hi = x.astype(BF16)
    r1 = x - hi.astype(F32)
    mid = r1.astype(BF16)
    lo = (r1 - mid.astype(F32)).astype(BF16)
    return hi, mid, lo


def _dot_exact_lhs(a01, x):
    hi, mid, lo = _split3(x)
    return _dot(a01, hi) + _dot(a01, mid) + _dot(a01, lo)


def _dot_exact_rhs(x, b01):
    hi, mid, lo = _split3(x)
    return _dot(hi, b01) + _dot(mid, b01) + _dot(lo, b01)


def _mod_kernel(c_ref, w_ref, b_ref, o_ref):
    s = _silu(c_ref[...])
    o_ref[0] = jnp.dot(s, w_ref[0], preferred_element_type=F32, precision=HIGHEST) + b_ref[0]


def _ada_mod(c, ada_w, ada_b):
    nb = c.shape[0]
    depth, _, f = ada_w.shape
    tn = 1536
    return pl.pallas_call(
        _mod_kernel,
        out_shape=jax.ShapeDtypeStruct((depth, nb, f), F32),
        grid=(depth, f // tn),
        in_specs=[
            pl.BlockSpec((nb, D), lambda l, j: (0, 0)),
            pl.BlockSpec((1, D, tn), lambda l, j: (l, 0, j)),
            pl.BlockSpec((1, 1, tn), lambda l, j: (l, 0, j)),
        ],
        out_specs=pl.BlockSpec((1, nb, tn), lambda l, j: (l, 0, j)),
        compiler_params=_cparams("parallel", "parallel"),
        name="ada_mod",
    )(c, ada_w, ada_b.reshape(depth, 1, f))


def _proj_kernel(tn, x_ref, g_ref, sc_ref, sh_ref, w_ref, o_ref):
    hn = _norm_mod(x_ref[0], g_ref[...], sc_ref[0], sh_ref[0]).astype(BF16)
    for j in range(w_ref.shape[1] // tn):
        cols = slice(j * tn, (j + 1) * tn)
        o_ref[0, :, cols] = _dot(hn, w_ref[:, cols]).astype(o_ref.dtype)


def _norm_proj(x, g, sc, sh, w, out_dtype, tm=512, tn=1024):
    b, t, _ = x.shape
    f = w.shape[1]
    tm = min(tm, t)
    tn = min(tn, f)
    return pl.pallas_call(
        functools.partial(_proj_kernel, tn),
        out_shape=jax.ShapeDtypeStruct((b, t, f), out_dtype),
        grid=(b, t // tm),
        in_specs=[
            pl.BlockSpec((1, tm, D), lambda bi, i: (bi, i, 0)),
            pl.BlockSpec((1, D), lambda bi, i: (0, 0)),
            pl.BlockSpec((1, 1, D), lambda bi, i: (bi, 0, 0)),
            pl.BlockSpec((1, 1, D), lambda bi, i: (bi, 0, 0)),
            pl.BlockSpec((D, f), lambda bi, i: (0, 0)),
        ],
        out_specs=pl.BlockSpec((1, tm, f), lambda bi, i: (bi, i, 0)),
        compiler_params=_cparams("parallel", "parallel"),
        name="norm_proj",
    )(x, g.reshape(1, D), sc, sh, w)


def _out_kernel(y_ref, w_ref, x_ref, g_ref, o_ref):
    o_ref[0] = x_ref[0] + g_ref[0] * _dot(y_ref[0], w_ref[...])


def _out_proj(y, w, x, gate, tm=512):
    b, t, k = y.shape
    tm = min(tm, t)
    return pl.pallas_call(
        _out_kernel,
        out_shape=jax.ShapeDtypeStruct((b, t, D), F32),
        grid=(b, t // tm),
        in_specs=[
            pl.BlockSpec((1, tm, k), lambda bi, i: (bi, i, 0)),
            pl.BlockSpec((k, D), lambda bi, i: (0, 0)),
            pl.BlockSpec((1, tm, D), lambda bi, i: (bi, i, 0)),
            pl.BlockSpec((1, 1, D), lambda bi, i: (bi, 0, 0)),
        ],
        out_specs=pl.BlockSpec((1, tm, D), lambda bi, i: (bi, i, 0)),
        compiler_params=_cparams("parallel", "parallel"),
        name="out_proj",
    )(y, w, x, gate)


def _route(sel, sc):
    n_in = N_EXPERTS // N_GROUPS
    gscore = []
    for g in range(N_GROUPS):
        a, b, c, d = sel[n_in * g:n_in * g + n_in]
        m1, n1 = jnp.maximum(a, b), jnp.minimum(a, b)
        m2, n2 = jnp.maximum(c, d), jnp.minimum(c, d)
        gscore.append(jnp.maximum(m1, m2) + jnp.maximum(jnp.minimum(m1, m2), jnp.maximum(n1, n2)))
    in_group = []
    taken = None
    for g in range(N_GROUPS):
        best = None
        for o in range(g + 1, N_GROUPS):
            c = gscore[g] >= gscore[o]
            best = c if best is None else (best & c)
        if best is None:
            best = jnp.ones_like(gscore[g], dtype=jnp.bool_)
        if taken is not None:
            best = best & jnp.logical_not(taken)
        in_group.append(best)
        taken = best if taken is None else (taken | best)
    vs, ts = [], []
    for k in range(n_in):
        v = sel[3 * n_in + k]
        t = sc[3 * n_in + k]
        for g in range(N_GROUPS - 2, -1, -1):
            v = jnp.where(in_group[g], sel[n_in * g + k], v)
            t = jnp.where(in_group[g], sc[n_in * g + k], t)
        vs.append(v)
        ts.append(t)
    chosen = []
    for i in range(n_in):
        beaten = None
        for j in range(n_in):
            if j == i:
                continue
            c = (vs[j] > vs[i]) if j > i else (vs[j] >= vs[i])
            c = c.astype(F32)
            beaten = c if beaten is None else beaten + c
        chosen.append(beaten < 1.5)
    denom = None
    for k in range(n_in):
        term = jnp.where(chosen[k], ts[k], 0.0)
        denom = term if denom is None else denom + term
    gates = []
    for g in range(N_GROUPS):
        for k in range(n_in):
            gates.append(jnp.where(in_group[g] & chosen[k], ts[k] / denom, 0.0))
    return gates, in_group


MOE_SUB = 256
MOE_ALIGN = 16
MOE_MAX_CHUNKS = 5


def _moe_route_kernel(x_ref, g_ref, sc_ref, sh_ref, wrt_ref, rb_ref,
                      xs_ref, gts_ref, pos_ref, flg_ref, gates_ref, grp_ref):
    tm = x_ref.shape[1]
    hn = _norm_mod(x_ref[0], g_ref[...], sc_ref[0], sh_ref[0])
    hn_hi = hn.astype(BF16)
    hn_mid = (hn - hn_hi.astype(F32)).astype(BF16)
    w_hi, w_mid, _ = _split3(wrt_ref[...])
    logits = _dot_nt(w_hi, hn_hi) + (_dot_nt(w_hi, hn_mid) + _dot_nt(w_mid, hn_hi))
    score = _sigmoid(logits)
    sel = score + rb_ref[...]
    gates, in_group = _route([sel[i:i + 1, :] for i in range(N_EXPERTS)],
                             [score[i:i + 1, :] for i in range(N_EXPERTS)])
    gates_ref[...] = jnp.zeros_like(gates_ref)
    for i in range(N_EXPERTS):
        gates_ref[i:i + 1, :] = gates[i]
    grp_ref[...] = jnp.zeros_like(grp_ref)
    for gi in range(N_GROUPS):
        grp_ref[gi:gi + 1, :] = jnp.where(in_group[gi], 1.0, 0.0)
    row = lax.broadcasted_iota(jnp.int32, (tm, tm), 0)
    colm = lax.broadcasted_iota(jnp.int32, (tm, tm), 1)
    tri = jnp.where(row <= colm, 1.0, 0.0).astype(BF16)
    cnt = _dot(grp_ref[...].astype(BF16), tri)
    sizes = [cnt[gi:gi + 1, tm - 1:tm] for gi in range(N_GROUPS)]
    offs = [jnp.zeros((1, 1), F32)]
    for gi in range(N_GROUPS - 1):
        offs.append(offs[-1] + sizes[gi])
    pos = jnp.zeros((1, tm), F32)
    for gi in range(N_GROUPS):
        pos = pos + jnp.where(in_group[gi], offs[gi] + cnt[gi:gi + 1, :] - 1.0, 0.0)
    perm = jnp.where(row.astype(F32) == pos, 1.0, 0.0).astype(BF16)
    xs_ref[0, :tm, :] = _dot(perm, hn_hi).astype(xs_ref.dtype)
    xs_ref[0, tm:, :] = jnp.zeros((MOE_SUB, D), xs_ref.dtype)
    hi, mid, _ = _split3(gates_ref[...])
    gts_ref[0, :tm, :] = _dot_nt(perm, hi) + _dot_nt(perm, mid)
    gts_ref[0, tm:, :] = jnp.zeros((MOE_SUB, V7X_LANES), F32)
    pos_ref[0] = pos.astype(jnp.int32)
    lane = lax.broadcasted_iota(jnp.int32, (1, V7X_LANES), 1)
    info = jnp.zeros((1, V7X_LANES), F32)
    for gi in range(N_GROUPS):
        start = jnp.floor(offs[gi] * (1.0 / MOE_ALIGN)) * MOE_ALIGN
        chunks = jnp.floor((offs[gi] + sizes[gi] - start + (MOE_SUB - 1.0)) * (1.0 / MOE_SUB))
        chunks = jnp.where(sizes[gi] > 0.0, chunks, 0.0)
        info = info + jnp.where(lane == gi, start, 0.0) + jnp.where(lane == N_GROUPS + gi, chunks, 0.0)
    flg_ref[0] = info.astype(jnp.int32)


def _moe_expert_kernel(flg_ref, xs_ref, gts_ref, pos_ref, x_ref, g2_ref, wg_ref, wu_ref, wd_ref, *rest):
    fg_ref = rest[0] if len(rest) == 3 else None
    o_ref, acc_ref = rest[-2:]
    i, e = pl.program_id(1), pl.program_id(2)
    tile = pl.program_id(0) * pl.num_programs(1) + i
    tm = x_ref.shape[1]

    @pl.when(e == 0)
    def _():
        acc_ref[...] = jnp.zeros_like(acc_ref)

    grp = e // (N_EXPERTS // N_GROUPS)
    start = flg_ref[tile, grp]
    chunks = flg_ref[tile, N_GROUPS + grp]
    lane = lax.broadcasted_iota(jnp.int32, (MOE_SUB, V7X_LANES), 1)
    for j in range(MOE_MAX_CHUNKS):
        @pl.when(j < chunks)
        def _():
            rows = pl.ds(pl.multiple_of(start + j * MOE_SUB, MOE_ALIGN), MOE_SUB)
            xb = xs_ref[0, rows, :]
            col = jnp.sum(jnp.where(lane == e, gts_ref[0, rows, :], 0.0), axis=1, keepdims=True)
            hid = _silu(_dot(xb, wg_ref[0])) * _dot(xb, wu_ref[0]) * col
            acc_ref[rows, :] += _dot(hid.astype(BF16), wd_ref[0])

    @pl.when(e == N_EXPERTS - 1)
    def _():
        row = lax.broadcasted_iota(jnp.int32, (tm, tm), 0)
        perm = jnp.where(row == pos_ref[0], 1.0, 0.0).astype(BF16)
        y = x_ref[0] + g2_ref[0] * _dot_tn(perm, acc_ref[:tm, :].astype(BF16))
        if fg_ref is not None:
            y = y * lax.rsqrt(jnp.mean(y * y, axis=-1, keepdims=True) + NORM_EPS) * fg_ref[...]
        o_ref[0] = y


def _moe(x, g, sc, sh, g2, wrt, rb, wg, wu, wd, final_g=None, tm=1024):
    b, t, _ = x.shape
    tm = min(tm, t)
    assert (tm + MOE_ALIGN - 1 + MOE_SUB - 1) // MOE_SUB <= MOE_MAX_CHUNKS
    nt = t // tm
    rp = tm + MOE_SUB
    tok = lambda bi, i: (bi, i, 0)
    per_tile = lambda bi, i: (bi * nt + i, 0, 0)
    per_b = lambda bi, i: (bi, 0, 0)
    fixed = lambda bi, i: (0, 0)
    xs, gts, pos, flg = pl.pallas_call(
        _moe_route_kernel,
        out_shape=[
            jax.ShapeDtypeStruct((b * nt, rp, D), BF16),
            jax.ShapeDtypeStruct((b * nt, rp, V7X_LANES), F32),
            jax.ShapeDtypeStruct((b * nt, 1, tm), jnp.int32),
            jax.ShapeDtypeStruct((b * nt, 1, V7X_LANES), jnp.int32),
        ],
        grid=(b, nt),
        in_specs=[
            pl.BlockSpec((1, tm, D), tok),
            pl.BlockSpec((1, D), fixed),
            pl.BlockSpec((1, 1, D), per_b),
            pl.BlockSpec((1, 1, D), per_b),
            pl.BlockSpec((N_EXPERTS, D), fixed),
            pl.BlockSpec((N_EXPERTS, 1), fixed),
        ],
        out_specs=[
            pl.BlockSpec((1, rp, D), per_tile),
            pl.BlockSpec((1, rp, V7X_LANES), per_tile),
            pl.BlockSpec((1, 1, tm), per_tile),
            pl.BlockSpec((1, 1, V7X_LANES), per_tile),
        ],
        scratch_shapes=[pltpu.VMEM((V7X_LANES, tm), F32), pltpu.VMEM((8, tm), F32)],
        compiler_params=_cparams("parallel", "parallel"),
        name="moe_route",
    )(x, g.reshape(1, D), sc, sh, wrt, rb)
    info = flg.reshape(b * nt, V7X_LANES)[:, :2 * N_GROUPS]
    tok3 = lambda bi, i, e, f: (bi, i, 0)
    tile3 = lambda bi, i, e, f: (bi * nt + i, 0, 0)
    return pl.pallas_call(
        _moe_expert_kernel,
        out_shape=jax.ShapeDtypeStruct((b, t, D), F32),
        grid_spec=pltpu.PrefetchScalarGridSpec(
            num_scalar_prefetch=1,
            grid=(b, nt, N_EXPERTS),
            in_specs=[
                pl.BlockSpec((1, rp, D), tile3),
                pl.BlockSpec((1, rp, V7X_LANES), tile3),
                pl.BlockSpec((1, 1, tm), tile3),
                pl.BlockSpec((1, tm, D), tok3),
                pl.BlockSpec((1, 1, D), lambda bi, i, e, f: (bi, 0, 0)),
                pl.BlockSpec((1, D, EXPERT_FF), lambda bi, i, e, f: (e, 0, 0)),
                pl.BlockSpec((1, D, EXPERT_FF), lambda bi, i, e, f: (e, 0, 0)),
                pl.BlockSpec((1, EXPERT_FF, D), lambda bi, i, e, f: (e, 0, 0)),
            ] + ([] if final_g is None else [pl.BlockSpec((1, D), lambda bi, i, e, f: (0, 0))]),
            out_specs=pl.BlockSpec((1, tm, D), tok3),
            scratch_shapes=[pltpu.VMEM((rp, D), F32)],
        ),
        compiler_params=_cparams("parallel", "parallel", "arbitrary"),
        name="moe_experts",
    )(info, xs, gts, pos, x, g2, wg, wu, wd, *([] if final_g is None else [final_g.reshape(1, D)]))


def _rotary(x, cos, sin):
    half = x.shape[-1] // 2
    x1, x2 = x[:, :half], x[:, half:]
    return jnp.concatenate([x1 * cos - x2 * sin, x1 * sin + x2 * cos], axis=-1)


def _ret_bwd_kernel(k_ref, v_ref, cos_ref, sin_ref, lg_ref, hb_ref, h_ref):
    c = pl.program_id(2)
    ch = k_ref.shape[1]

    @pl.when(c == 0)
    def _():
        h_ref[...] = jnp.zeros_like(h_ref)

    hb_ref[0, 0, 0] = h_ref[...]
    lg = lg_ref[0][:, :1]
    pos = lax.broadcasted_iota(jnp.int32, (ch, 1), 0).astype(F32)
    kr = _rotary(k_ref[0].astype(F32), cos_ref[...], sin_ref[...]) * (RET_DK ** -0.5)
    kw = (kr * jnp.exp(lg * (pos + 1.0))).astype(BF16)
    h_ref[...] = jnp.exp(lg * ch) * h_ref[...] + _dot_tn(kw, v_ref[0])


def _ret_fwd_kernel(q_ref, k_ref, v_ref, gate_ref, cos_ref, sin_ref, lg_ref, hb_ref, gg_ref, gb_ref,
                    o_ref, h_ref, dec_ref):
    c = pl.program_id(2)
    ch = q_ref.shape[1]
    lg = lg_ref[0][:, :1]

    @pl.when(c == 0)
    def _():
        h_ref[...] = jnp.zeros_like(h_ref)
        ii = lax.broadcasted_iota(jnp.int32, (ch, ch), 0)
        jj = lax.broadcasted_iota(jnp.int32, (ch, ch), 1)
        dec_ref[...] = jnp.exp(lg * jnp.abs(ii - jj).astype(F32))

    cos, sin = cos_ref[...], sin_ref[...]
    pos = lax.broadcasted_iota(jnp.int32, (ch, 1), 0).astype(F32)
    qr = _rotary(q_ref[0].astype(F32), cos, sin)
    kr = _rotary(k_ref[0].astype(F32), cos, sin) * (RET_DK ** -0.5)
    v = v_ref[0]
    s = _dot_nt(qr.astype(BF16), kr.astype(BF16)) * dec_ref[...]
    y = _dot(s.astype(BF16), v)
    q2 = jnp.concatenate([qr * jnp.exp(lg * (pos + 1.0)), qr * jnp.exp(lg * (ch - 1.0 - pos))], axis=-1)
    h2 = jnp.concatenate([h_ref[...], hb_ref[0, 0, 0]], axis=0)
    y = y + _dot(q2.astype(BF16), h2.astype(BF16))
    kw = (kr * jnp.exp(lg * (ch - 1.0 - pos))).astype(BF16)
    h_ref[...] = jnp.exp(lg * ch) * h_ref[...] + _dot_tn(kw, v)
    mu = jnp.mean(y, axis=-1, keepdims=True)
    var = jnp.mean(jnp.square(y - mu), axis=-1, keepdims=True)
    yn = (y - mu) * lax.rsqrt(var + RET_GN_EPS) * gg_ref[...] + gb_ref[...]
    gate = gate_ref[0]
    o_ref[0] = (_silu(gate) * yn.astype(BF16)).astype(o_ref.dtype)


def _retention(proj, cos, sin, lg, gn_g, gn_b, ch=512):
    b, t, _ = proj.shape
    ch = min(ch, t)
    nc = t // ch
    kq = RET_QK // RET_DK
    vq = 2 * RET_QK // RET_DV
    gq = vq + RET_HEADS
    hb = pl.pallas_call(
        _ret_bwd_kernel,
        out_shape=jax.ShapeDtypeStruct((b, RET_HEADS, nc, RET_DK, RET_DV), F32),
        grid=(b, RET_HEADS, nc),
        in_specs=[
            pl.BlockSpec((1, ch, RET_DK), lambda bi, h, c: (bi, nc - 1 - c, kq + h)),
            pl.BlockSpec((1, ch, RET_DV), lambda bi, h, c: (bi, nc - 1 - c, vq + h)),
            pl.BlockSpec((ch, RET_DK // 2), lambda bi, h, c: (nc - 1 - c, 0)),
            pl.BlockSpec((ch, RET_DK // 2), lambda bi, h, c: (nc - 1 - c, 0)),
            pl.BlockSpec((1, 1, V7X_LANES), lambda bi, h, c: (h, 0, 0)),
        ],
        out_specs=pl.BlockSpec((1, 1, 1, RET_DK, RET_DV), lambda bi, h, c: (bi, h, nc - 1 - c, 0, 0)),
        scratch_shapes=[pltpu.VMEM((RET_DK, RET_DV), F32)],
        compiler_params=_cparams("parallel", "parallel", "arbitrary"),
        name="ret_bwd_state",
    )(proj, proj, cos, sin, lg)
    return pl.pallas_call(
        _ret_fwd_kernel,
        out_shape=jax.ShapeDtypeStruct((b, t, RET_V), BF16),
        grid=(b, RET_HEADS, nc),
        in_specs=[
            pl.BlockSpec((1, ch, RET_DK), lambda bi, h, c: (bi, c, h)),
            pl.BlockSpec((1, ch, RET_DK), lambda bi, h, c: (bi, c, kq + h)),
            pl.BlockSpec((1, ch, RET_DV), lambda bi, h, c: (bi, c, vq + h)),
            pl.BlockSpec((1, ch, RET_DV), lambda bi, h, c: (bi, c, gq + h)),
            pl.BlockSpec((ch, RET_DK // 2), lambda bi, h, c: (c, 0)),
            pl.BlockSpec((ch, RET_DK // 2), lambda bi, h, c: (c, 0)),
            pl.BlockSpec((1, 1, V7X_LANES), lambda bi, h, c: (h, 0, 0)),
            pl.BlockSpec((1, 1, 1, RET_DK, RET_DV), lambda bi, h, c: (bi, h, c, 0, 0)),
            pl.BlockSpec((1, RET_DV), lambda bi, h, c: (0, h)),
            pl.BlockSpec((1, RET_DV), lambda bi, h, c: (0, h)),
        ],
        out_specs=pl.BlockSpec((1, ch, RET_DV), lambda bi, h, c: (bi, c, h)),
        scratch_shapes=[pltpu.VMEM((RET_DK, RET_DV), F32), pltpu.VMEM((ch, ch), F32)],
        compiler_params=_cparams("parallel", "parallel", "arbitrary"),
        name="ret_fwd",
    )(proj, proj, proj, proj, cos, sin, lg, hb, gn_g.reshape(1, RET_V), gn_b.reshape(1, RET_V))


def _ret_tables(t):
    half = RET_DK // 2
    inv = RET_ROPE_BASE ** (-jnp.arange(half, dtype=F32) / half)
    ang = jnp.arange(t, dtype=F32)[:, None] * inv[None, :]
    log_gamma = jnp.log(1.0 - 2.0 ** (-5.0 - jnp.arange(RET_HEADS, dtype=F32)))
    lg = jnp.broadcast_to(log_gamma[:, None, None], (RET_HEADS, 1, V7X_LANES))
    return jnp.cos(ang), jnp.sin(ang), lg


def _ret_layer(x, g, sc, sh, gate, tables, w_in, gn_g, gn_b, w_out):
    proj = _norm_proj(x, g, sc, sh, w_in.astype(BF16), BF16)
    y = _retention(proj, *tables, gn_g, gn_b)
    return _out_proj(y, w_out.astype(BF16), x, gate)


CONV_HALO = 16


def _conv_kernel(cur_ref, prev_ref, next_ref, w_ref, b_ref, o_ref):
    i = pl.program_id(1)
    tt = cur_ref.shape[1]
    cur = cur_ref[0].astype(F32)
    prev = jnp.where(i == 0, 0.0, prev_ref[0].astype(F32))
    nxt = jnp.where(i == pl.num_programs(1) - 1, 0.0, next_ref[0].astype(F32))
    ext = jnp.concatenate([prev, cur, nxt], axis=0)
    w = w_ref[...]
    acc = jnp.zeros_like(cur) + b_ref[...]
    for s in range(SSM_CONV):
        off = CONV_HALO + s - SSM_CONV // 2
        acc = acc + w[s:s + 1, :] * ext[off:off + tt]
    o_ref[0] = _silu(acc).astype(o_ref.dtype)


def _ssd_conv(zx, conv_w, conv_b, tt=512, tc=1024):
    b, t, _ = zx.shape
    tt = min(tt, t)
    off = SSM_INNER // tc
    nh = tt // CONV_HALO
    last = t // CONV_HALO - 1
    w = jnp.zeros((8, SSM_CONV_DIM), F32).at[:SSM_CONV].set(conv_w)
    return pl.pallas_call(
        _conv_kernel,
        out_shape=jax.ShapeDtypeStruct((b, t, SSM_CONV_DIM), BF16),
        grid=(b, t // tt, SSM_CONV_DIM // tc),
        in_specs=[
            pl.BlockSpec((1, tt, tc), lambda bi, i, j: (bi, i, off + j)),
            pl.BlockSpec((1, CONV_HALO, tc), lambda bi, i, j: (bi, jnp.maximum(i * nh - 1, 0), off + j)),
            pl.BlockSpec((1, CONV_HALO, tc), lambda bi, i, j: (bi, jnp.minimum((i + 1) * nh, last), off + j)),
            pl.BlockSpec((8, tc), lambda bi, i, j: (0, j)),
            pl.BlockSpec((1, tc), lambda bi, i, j: (0, j)),
        ],
        out_specs=pl.BlockSpec((1, tt, tc), lambda bi, i, j: (bi, i, j)),
        compiler_params=_cparams("parallel", "parallel", "parallel"),
        name="ssd_conv",
    )(zx, zx, zx, w, conv_b.reshape(1, SSM_CONV_DIM))


def _ssd_decays(dt_ref, bias_ref, alog_ref):
    ch = dt_ref.shape[1]
    dt = _softplus(dt_ref[0] + bias_ref[...])
    la = dt * (-jnp.exp(alog_ref[...]))
    ii = lax.broadcasted_iota(jnp.int32, (ch, ch), 0)
    jj = lax.broadcasted_iota(jnp.int32, (ch, ch), 1)
    tri = jnp.where(ii >= jj, 1.0, 0.0).astype(BF16)
    cum = _dot_exact_lhs(tri, la)
    return dt, la, cum


def _pair_cols(lo, arr, h0):
    return jnp.where(lo, arr[:, h0:h0 + 1], arr[:, h0 + 1:h0 + 2])


def _ssd_bwd_kernel(xs_ref, b_ref, dt_ref, bias_ref, alog_ref, hb_ref, dtv_ref, cum_ref, h_ref):
    c = pl.program_id(1)
    ch = xs_ref.shape[1]

    @pl.when(c == 0)
    def _():
        h_ref[...] = jnp.zeros_like(h_ref)

    hb_ref[0, 0] = h_ref[...]
    dt, la, cum = _ssd_decays(dt_ref, bias_ref, alog_ref)
    dtv_ref[0] = dt
    cum_ref[0] = cum
    wb = jnp.exp(cum - la) * dt
    etot = jnp.exp(cum[ch - 1:ch, :])
    lo = lax.broadcasted_iota(jnp.int32, (ch, V7X_LANES), 1) < SSM_HEADDIM
    for p in range(SSM_PAIRS):
        hb0 = SSM_HEADS + 2 * p
        g = p // (SSM_PAIRS // SSM_GROUPS)
        sl = slice(p * V7X_LANES, (p + 1) * V7X_LANES)
        xw = (xs_ref[0, :, sl].astype(F32) * _pair_cols(lo, wb, hb0)).astype(BF16)
        bg = b_ref[0, :, g * SSM_STATE:(g + 1) * SSM_STATE]
        h_ref[p] = _pair_cols(lo[:1], etot, hb0) * h_ref[p] + _dot_tn(bg, xw)


def _ssd_fwd_kernel(z_ref, xs_ref, b_ref, c_ref, dtv_ref, cum_ref, alog_ref, hb_ref, dsk_ref, ng_ref,
                    o_ref, h_ref, y_ref):
    c = pl.program_id(1)
    ch = xs_ref.shape[1]

    @pl.when(c == 0)
    def _():
        h_ref[...] = jnp.zeros_like(h_ref)

    dt, cum = dtv_ref[0], cum_ref[0]
    cbx = cum - dt * (-jnp.exp(alog_ref[...]))
    tot = cum[ch - 1:ch, :]
    cum_t, cbx_t, dt_t = cum.T, cbx.T, dt.T
    ef = jnp.exp(cum)
    eb = jnp.exp(tot - cbx)
    wf = jnp.exp(tot - cum) * dt
    etot = jnp.exp(tot)
    lo = lax.broadcasted_iota(jnp.int32, (ch, V7X_LANES), 1) < SSM_HEADDIM
    lower = (lax.broadcasted_iota(jnp.int32, (ch, ch), 0) >= lax.broadcasted_iota(jnp.int32, (ch, ch), 1))
    ppg = SSM_PAIRS // SSM_GROUPS
    for g in range(SSM_GROUPS):
        gs = slice(g * SSM_STATE, (g + 1) * SSM_STATE)
        cg = c_ref[0, :, gs]
        bg = b_ref[0, :, gs]
        cb = _dot_nt(cg, bg)
        for q in range(ppg):
            p = g * ppg + q
            sl = slice(p * V7X_LANES, (p + 1) * V7X_LANES)
            x = xs_ref[0, :, sl]
            xf = x.astype(F32)
            y = xf * dsk_ref[:, sl]
            for hh in range(2):
                hf = 2 * p + hh
                hbw = SSM_HEADS + hf
                arg = jnp.where(lower, cum[:, hf:hf + 1] - cum_t[hf:hf + 1, :],
                                cbx_t[hbw:hbw + 1, :] - cbx[:, hbw:hbw + 1])
                m = cb * jnp.exp(arg) * jnp.where(lower, dt_t[hf:hf + 1, :], dt_t[hbw:hbw + 1, :])
                xm = jnp.where(lo if hh == 0 else jnp.logical_not(lo), x, jnp.zeros_like(x))
                y = y + _dot(m.astype(BF16), xm)
            hprev = h_ref[p]
            y = y + _pair_cols(lo, ef, 2 * p) * _dot(cg, hprev.astype(BF16))
            y = y + _pair_cols(lo, eb, SSM_HEADS + 2 * p) * _dot(cg, hb_ref[0, 0, p].astype(BF16))
            y_ref[:, sl] = y * _silu(z_ref[0, :, sl].astype(F32))
            xw = (xf * _pair_cols(lo, wf, 2 * p)).astype(BF16)
            h_ref[p] = _pair_cols(lo[:1], etot, 2 * p) * hprev + _dot_tn(bg, xw)
    gw = SSM_INNER // SSM_GROUPS
    for g in range(SSM_GROUPS):
        gs = slice(g * gw, (g + 1) * gw)
        yg = y_ref[:, gs]
        ms = jnp.mean(yg * yg, axis=-1, keepdims=True)
        o_ref[0, :, gs] = (yg * lax.rsqrt(ms + NORM_EPS) * ng_ref[:, gs]).astype(o_ref.dtype)


def _ssd_scan(zx, xbc, dtr, bias, alog, dsk, ng):
    b, t, _ = xbc.shape
    ch = SSM_CHUNK
    nc = t // ch
    gn = SSM_GROUPS * SSM_STATE
    bq = SSM_INNER // gn
    vec = lambda bi, c: (0, 0)
    lanes_rev = pl.BlockSpec((1, ch, V7X_LANES), lambda bi, c: (bi, nc - 1 - c, 0))
    hb, dtv, cum = pl.pallas_call(
        _ssd_bwd_kernel,
        out_shape=[jax.ShapeDtypeStruct((b, nc, SSM_PAIRS, SSM_STATE, V7X_LANES), F32),
                   jax.ShapeDtypeStruct((b, t, V7X_LANES), F32), jax.ShapeDtypeStruct((b, t, V7X_LANES), F32)],
        grid=(b, nc),
        in_specs=[
            pl.BlockSpec((1, ch, SSM_INNER), lambda bi, c: (bi, nc - 1 - c, 0)),
            pl.BlockSpec((1, ch, gn), lambda bi, c: (bi, nc - 1 - c, bq)),
            pl.BlockSpec((1, ch, V7X_LANES), lambda bi, c: (bi, nc - 1 - c, 0)),
            pl.BlockSpec((1, V7X_LANES), vec),
            pl.BlockSpec((1, V7X_LANES), vec),
        ],
        out_specs=[pl.BlockSpec((1, 1, SSM_PAIRS, SSM_STATE, V7X_LANES), lambda bi, c: (bi, nc - 1 - c, 0, 0, 0)),
                   lanes_rev, lanes_rev],
        scratch_shapes=[pltpu.VMEM((SSM_PAIRS, SSM_STATE, V7X_LANES), F32)],
        compiler_params=_cparams("parallel", "arbitrary"),
        name="ssd_bwd_state",
    )(xbc, xbc, dtr, bias, alog)
    return pl.pallas_call(
        _ssd_fwd_kernel,
        out_shape=jax.ShapeDtypeStruct((b, t, SSM_INNER), BF16),
        grid=(b, nc),
        in_specs=[
            pl.BlockSpec((1, ch, SSM_INNER), lambda bi, c: (bi, c, 0)),
            pl.BlockSpec((1, ch, SSM_INNER), lambda bi, c: (bi, c, 0)),
            pl.BlockSpec((1, ch, gn), lambda bi, c: (bi, c, bq)),
            pl.BlockSpec((1, ch, gn), lambda bi, c: (bi, c, bq + 1)),
            pl.BlockSpec((1, ch, V7X_LANES), lambda bi, c: (bi, c, 0)),
            pl.BlockSpec((1, ch, V7X_LANES), lambda bi, c: (bi, c, 0)),
            pl.BlockSpec((1, V7X_LANES), vec),
            pl.BlockSpec((1, 1, SSM_PAIRS, SSM_STATE, V7X_LANES), lambda bi, c: (bi, c, 0, 0, 0)),
            pl.BlockSpec((1, SSM_INNER), vec),
            pl.BlockSpec((1, SSM_INNER), vec),
        ],
        out_specs=pl.BlockSpec((1, ch, SSM_INNER), lambda bi, c: (bi, c, 0)),
        scratch_shapes=[pltpu.VMEM((SSM_PAIRS, SSM_STATE, V7X_LANES), F32), pltpu.VMEM((ch, SSM_INNER), F32)],
        compiler_params=_cparams("parallel", "arbitrary"),
        name="ssd_fwd",
    )(zx, xbc, xbc, xbc, dtv, cum, alog, hb, dsk, ng)


def _pad_lanes(v):
    v = v.reshape(1, -1)
    return jnp.pad(v, ((0, 0), (0, V7X_LANES - v.shape[1])))


def _ssd_layer(x, g, sc, sh, gate, w_in, conv_w, conv_b, dt_bias, a_log, d_skip, norm_g, w_out):
    n_main = SSM_INNER + SSM_CONV_DIM
    w_main = w_in[:, :n_main].astype(BF16)
    w_dt = jnp.pad(w_in[:, n_main:], ((0, 0), (0, V7X_LANES - 2 * SSM_HEADS))).astype(BF16)
    zx = _norm_proj(x, g, sc, sh, w_main, BF16)
    dtr = _norm_proj(x, g, sc, sh, w_dt, F32)
    xbc = _ssd_conv(zx, conv_w, conv_b)
    dsk = jnp.repeat(d_skip, SSM_HEADDIM).reshape(1, SSM_INNER)
    y = _ssd_scan(zx, xbc, dtr, _pad_lanes(dt_bias), _pad_lanes(a_log), dsk, norm_g.reshape(1, SSM_INNER))
    return _out_proj(y, w_out.astype(BF16), x, gate)


def _head_sums(x, ones_bd, passes=3):
    parts = _split3(x)[:passes]
    acc = _dot(parts[0], ones_bd)
    for part in parts[1:]:
        acc = acc + _dot(part, ones_bd)
    return acc


def _rwkv_prep_kernel(x_ref, xp_ref, xn_ref, ng_ref, sc_ref, sh_ref, mu_ref, wr_ref, wk_ref, wv_ref,
                      dw1_ref, dw2_ref, a1_ref, a2_ref, g1_ref, g2_ref, w0_ref, a0_ref, kk_ref, ka_ref, ones_ref,
                      r_o, v_o, kn_o, g_o, lwf_o, lwb_o, kdf_o, kdb_o, bf_o, bb_o):
    i = pl.program_id(1)
    norm = lambda xr: _norm_mod(xr, ng_ref[...], sc_ref[0], sh_ref[0])
    h = norm(x_ref[0])
    tt = h.shape[0]
    prev_row = jnp.where(i == 0, 0.0, norm(xp_ref[0])[7:8, :])
    next_row = jnp.where(i == pl.num_programs(1) - 1, 0.0, norm(xn_ref[0])[0:1, :])
    row = lax.broadcasted_iota(jnp.int32, (tt, 1), 0)
    up = jnp.where(row == 0, prev_row, pltpu.roll(h, 1, axis=0))
    dn = jnp.where(row == tt - 1, next_row, pltpu.roll(h, tt - 1, axis=0))
    xx = 0.5 * (up + dn) - h
    mu = mu_ref[...]
    xr, xw, xk, xv, xa, xg = [(h + xx * mu[j:j + 1, :]).astype(BF16) for j in range(6)]
    r = _dot(xr, wr_ref[...])
    k = _dot(xk, wk_ref[...])
    v = _dot(xv, wv_ref[...])
    g = _dot(_sigmoid(_dot(xg, g1_ref[...])).astype(BF16), g2_ref[...])
    wlo = _dot(jnp.tanh(_dot(xw, dw1_ref[...])).astype(BF16), dw2_ref[...])
    alo = _dot(_dot(xa, a1_ref[...]).astype(BF16), a2_ref[...])
    kkf = k * kk_ref[...]
    ones_bd = ones_ref[...]
    kn_parts = []
    for p in range(RWKV_PAIRS):
        sl = slice(p * V7X_LANES, (p + 1) * V7X_LANES)
        kp = kkf[:, sl]
        kn_parts.append(kp * lax.rsqrt(_head_sums(kp * kp, ones_bd, 2) + 1e-12))
    kn = jnp.concatenate(kn_parts, axis=-1)
    r_o[0] = r.astype(r_o.dtype)
    v_o[0] = v.astype(v_o.dtype)
    kn_o[0] = kn.astype(kn_o.dtype)
    g_o[0] = g.astype(g_o.dtype)
    ka = ka_ref[...]
    for d, (lw_o, kd_o, b_o) in enumerate(((lwf_o, kdf_o, bf_o), (lwb_o, kdb_o, bb_o))):
        w_raw = w0_ref[d:d + 1, :] + wlo[:, d * D:(d + 1) * D]
        lw_o[0] = -math.exp(-0.5) * _sigmoid(w_raw)
        a = _sigmoid(a0_ref[d:d + 1, :] + alo[:, d * D:(d + 1) * D])
        kd_o[0] = (k * (1.0 + (a - 1.0) * ka)).astype(kd_o.dtype)
        b_o[0] = (kn * a).astype(b_o.dtype)


def _block_diag2(w):
    z = jnp.zeros_like(w[0])
    return jnp.concatenate([jnp.concatenate([w[0], z], axis=1), jnp.concatenate([z, w[1]], axis=1)], axis=0)


def _rwkv_prep(x, ng, sc, sh, mu, w_r, w_k, w_v, dw0, dw1, dw2, a0, a1, a2, g1, g2, k_k, k_a, ones_bd, tt=256):
    b, t, _ = x.shape
    tt = min(tt, t)
    nh = tt // 8
    last = t // 8 - 1
    cat2 = lambda w: jnp.concatenate([w[0], w[1]], axis=1)
    mu8 = jnp.zeros((8, D), F32).at[:mu.shape[0]].set(mu)
    consts = [mu8, w_r.astype(BF16), w_k.astype(BF16), w_v.astype(BF16),
              cat2(dw1).astype(BF16), _block_diag2(dw2).astype(BF16), cat2(a1).astype(BF16), _block_diag2(a2).astype(BF16),
              g1.astype(BF16), g2.astype(BF16), dw0, a0, k_k.reshape(1, D), k_a.reshape(1, D), ones_bd]
    ins = [x, x, x, ng.reshape(1, D), sc, sh] + consts
    tok = lambda bi, i: (bi, i, 0)
    per_b = lambda bi, i: (bi, 0, 0)
    full = lambda a: pl.BlockSpec(a.shape, lambda bi, i: (0,) * a.ndim)
    in_specs = [
        pl.BlockSpec((1, tt, D), tok),
        pl.BlockSpec((1, 8, D), lambda bi, i: (bi, jnp.maximum(i * nh - 1, 0), 0)),
        pl.BlockSpec((1, 8, D), lambda bi, i: (bi, jnp.minimum((i + 1) * nh, last), 0)),
        pl.BlockSpec((1, D), lambda bi, i: (0, 0)),
        pl.BlockSpec((1, 1, D), per_b),
        pl.BlockSpec((1, 1, D), per_b),
    ] + [full(a) for a in consts]
    sds = lambda dt: jax.ShapeDtypeStruct((b, t, D), dt)
    out_dtypes = [BF16, BF16, BF16, BF16, F32, F32, BF16, BF16, BF16, BF16]
    return pl.pallas_call(
        _rwkv_prep_kernel,
        out_shape=[sds(dt) for dt in out_dtypes],
        grid=(b, t // tt),
        in_specs=in_specs,
        out_specs=[pl.BlockSpec((1, tt, D), tok) for _ in out_dtypes],
        compiler_params=_cparams("parallel", "parallel"),
        name="rwkv_prep",
    )(*ins)


def _unit_tri_inverse(a_list):
    n = a_list[0].shape[0]
    eye = jnp.where(lax.broadcasted_iota(jnp.int32, (n, n), 0) == lax.broadcasted_iota(jnp.int32, (n, n), 1), 1.0, 0.0)
    mm = lambda u, w: _dot(u.astype(BF16), w.astype(BF16))
    xs = [eye - a for a in a_list]
    pws = [mm(a, a) for a in a_list]
    k = 2
    while k < RWKV_CHUNK:
        xs = [x + mm(x, pw) for x, pw in zip(xs, pws)]
        k *= 2
        if k < RWKV_CHUNK:
            pws = [mm(pw, pw) for pw in pws]
    return xs


RWKV_STACK_HEADS = 2
RWKV_STACK_W = RWKV_STACK_HEADS * RWKV_HEAD
RWKV_STACKS = RWKV_HEADS // RWKV_STACK_HEADS


def _rwkv_scan_kernel(reverse, final, r_ref, kn_ref, v_ref, kd_ref, b_ref, lw_ref, *rest):
    if final:
        yb_ref, g_ref, rk_ref, gg_ref, gb_ref, ones_ref, o_ref, s_ref = rest
    else:
        o_ref, s_ref = rest
    c = pl.program_id(1)
    ch = RWKV_CHUNK
    nh, w = RWKV_STACK_HEADS, RWKV_STACK_W
    nb = r_ref.shape[0]

    @pl.when(c == 0)
    def _():
        s_ref[...] = jnp.zeros_like(s_ref)

    ii = lax.broadcasted_iota(jnp.int32, (ch, ch), 0)
    jj = lax.broadcasted_iota(jnp.int32, (ch, ch), 1)
    tri = jnp.where((ii <= jj) if reverse else (ii >= jj), 1.0, 0.0).astype(BF16)

    n = nh * ch
    i2 = lax.broadcasted_iota(jnp.int32, (n, n), 0)
    j2 = lax.broadcasted_iota(jnp.int32, (n, n), 1)
    same = (i2 // ch) == (j2 // ch)
    ti, tj = i2 % ch, j2 % ch
    strict = same & ((ti < tj) if reverse else (ti > tj))
    incl = strict if reverse else (same & (ti >= tj))
    lane_head = lax.broadcasted_iota(jnp.int32, (ch, w), 1) // RWKV_HEAD

    def stk(a, sl):
        ap = a[:, sl]
        return jnp.concatenate([jnp.where(lane_head == k, ap, 0.0) for k in range(nh)], axis=0).astype(BF16)

    sls = [slice(p * w, (p + 1) * w) for p in range(RWKV_STACKS)]
    kqs, rts, kdts, bts, kdls, bls, vs, etots = [], [], [], [], [], [], [], []
    rows = []
    for bi in range(nb):
        lw = lw_ref[bi]
        cum = _dot_exact_lhs(tri, lw)
        cum_ex = cum - lw
        tot = cum[0:1, :] if reverse else cum[ch - 1:ch, :]
        r = r_ref[bi].astype(F32)
        kn = kn_ref[bi].astype(F32)
        kd = kd_ref[bi].astype(F32)
        bb = b_ref[bi].astype(F32)
        v = v_ref[bi].astype(F32)
        e_ex = jnp.exp(cum_ex)
        e_neg = jnp.exp(-cum)
        e_end = jnp.exp(tot - cum)
        kq = kn * e_ex
        rt = r * (e_ex if reverse else jnp.exp(cum))
        kdt = kd * e_neg
        bt = bb * e_neg
        kdl = kd * e_end
        bl = bb * e_end
        etot = jnp.exp(tot)
        rows.append((r, kd, v))
        for sl in sls:
            kqs.append(stk(kq, sl))
            rts.append(stk(rt, sl))
            kdts.append(stk(kdt, sl))
            bts.append(stk(bt, sl))
            kdls.append(stk(kdl, sl))
            bls.append(stk(bl, sl))
            vs.append(stk(v, sl))
            etots.append(etot[:, sl])
    nprob = nb * RWKV_STACKS
    quads = [_dot_nt(jnp.concatenate([x, y], axis=0), jnp.concatenate([kd_, bq], axis=0))
             for x, y, kd_, bq in zip(kqs, rts, kdts, bts)]
    a2 = [jnp.where(strict, q[:n, n:], 0.0) for q in quads]
    tinv = _unit_tri_inverse(a2)
    a1 = [jnp.where(strict, q[:n, :n], 0.0).astype(BF16) for q in quads]
    ss = [s_ref[p] for p in range(nprob)]
    sbs = [s.astype(BF16) for s in ss]
    rhs = [_dot_nt(x, sb) + _dot(a, vv) for x, sb, a, vv in zip(kqs, sbs, a1, vs)]
    zs = [_dot(t.astype(BF16), q.astype(BF16)) for t, q in zip(tinv, rhs)]
    zbs = [z.astype(BF16) for z in zs]
    nzbs = [(-z).astype(BF16) for z in zs]
    a34 = [jnp.concatenate([jnp.where(incl, q[n:, :n], 0.0), jnp.where(incl, -q[n:, n:], 0.0)], axis=1).astype(BF16)
           for q in quads]
    yss = [_dot_nt(x, sb) + _dot(a, jnp.concatenate([vv, zb], axis=0))
           for x, sb, a, vv, zb in zip(rts, sbs, a34, vs, zbs)]
    for p, (s, et, vv, kk, nzb, bq) in enumerate(zip(ss, etots, vs, kdls, nzbs, bls)):
        s_ref[p] = s * et + _dot_tn(jnp.concatenate([vv, nzb], axis=0), jnp.concatenate([kk, bq], axis=0))
    for p, ys in enumerate(yss):
        bi, sl = p // RWKV_STACKS, sls[p % RWKV_STACKS]
        y = ys[:ch]
        for k in range(1, nh):
            y = y + ys[k * ch:(k + 1) * ch]
        if not final:
            o_ref[bi, :, sl] = y
        else:
            r, kd, v = rows[bi]
            ones_bd = ones_ref[...]
            ysum = y + yb_ref[bi, :, sl]
            mean = _head_sums(ysum, ones_bd, 2) * (1.0 / RWKV_HEAD)
            dev = ysum - mean
            var = _head_sums(dev * dev, ones_bd, 1) * (1.0 / RWKV_HEAD)
            yn = dev * lax.rsqrt(var + RWKV_GN_EPS) * gg_ref[:, sl] + gb_ref[:, sl]
            bonus = _head_sums(r[:, sl] * kd[:, sl] * rk_ref[:, sl], ones_bd, 1) * v[:, sl]
            o_ref[bi, :, sl] = ((yn + bonus) * g_ref[bi, :, sl].astype(F32)).astype(o_ref.dtype)


def _rwkv_scan(reverse, r, kn, v, kd, bb, lw, extra=None):
    b, t, _ = r.shape
    ch = RWKV_CHUNK
    nc = t // ch
    nb = 2 if b % 2 == 0 else 1
    tok = (lambda bi, c: (bi, nc - 1 - c, 0)) if reverse else (lambda bi, c: (bi, c, 0))
    vec = lambda bi, c: (0, 0)
    blk = pl.BlockSpec((nb, ch, D), tok)
    ins = [r, kn, v, kd, bb, lw]
    in_specs = [blk] * 6
    final = extra is not None
    if final:
        yb, g, rk, gg, gb, ones_bd = extra
        ins += [yb, g, rk, gg, gb, ones_bd]
        in_specs += [blk, blk, pl.BlockSpec((1, D), vec), pl.BlockSpec((1, D), vec), pl.BlockSpec((1, D), vec),
                     pl.BlockSpec((RWKV_STACK_W, RWKV_STACK_W), vec)]
    return pl.pallas_call(
        functools.partial(_rwkv_scan_kernel, reverse, final),
        out_shape=jax.ShapeDtypeStruct((b, t, D), BF16 if final else F32),
        grid=(b // nb, nc),
        in_specs=in_specs,
        out_specs=blk,
        scratch_shapes=[pltpu.VMEM((nb * RWKV_STACKS, RWKV_STACK_W, RWKV_STACK_W), F32)],
        compiler_params=_cparams("parallel", "arbitrary"),
        name="rwkv_scan_bwd" if reverse else "rwkv_scan_fwd",
    )(*ins)


def _rwkv_layer(x, g, sc, sh, gate, mu, w_r, w_k, w_v, w_o, dw0, dw1, dw2, a0, a1, a2, g1, g2,
                k_k, k_a, r_k, gn_g, gn_b):
    lane = jnp.arange(RWKV_STACK_W) // RWKV_HEAD
    ones_w = (lane[:, None] == lane[None, :]).astype(BF16)
    ones_bd = ones_w[:V7X_LANES, :V7X_LANES]
    r, v, kn, gg, lwf, lwb, kdf, kdb, bf, bb = _rwkv_prep(x, g, sc, sh, mu, w_r, w_k, w_v, dw0, dw1, dw2, a0, a1, a2,
                                                          g1, g2, k_k, k_a, ones_bd)
    yb = _rwkv_scan(True, r, kn, v, kdb, bb, lwb)
    y = _rwkv_scan(False, r, kn, v, kdf, bf, lwf,
                   extra=(yb, gg, r_k.reshape(1, D), gn_g.reshape(1, D), gn_b.reshape(1, D), ones_w))
    return _out_proj(y, w_o.astype(BF16), x, gate)


def _run_trunk(x, mod, w):
    tables = _ret_tables(x.shape[1])
    wrt = w['moe_w_router'].T
    rb = w['moe_router_bias'].reshape(N_EXPERTS, 1)
    for i in range(DEPTH):
        sh1, sc1, g1, sh2, sc2, g2 = [m[:, None, :] for m in jnp.split(mod[i], ADA_CHUNKS, axis=-1)]
        kind, j = i % 3, i // 3
        ng = w['norm_mix_g'][i]
        if kind == 0:
            x = _ret_layer(x, ng, sc1, sh1, g1, tables, w['ret_w_in'][j], w['ret_gn_g'][j], w['ret_gn_b'][j],
                           w['ret_w_out'][j])
        elif kind == 1:
            x = _rwkv_layer(x, ng, sc1, sh1, g1, w['rwkv_mu'][j], w['rwkv_w_r'][j], w['rwkv_w_k'][j],
                            w['rwkv_w_v'][j], w['rwkv_w_o'][j], w['rwkv_decay_w0'][j], w['rwkv_decay_w1'][j],
                            w['rwkv_decay_w2'][j], w['rwkv_iclr_a0'][j], w['rwkv_iclr_a1'][j], w['rwkv_iclr_a2'][j],
                            w['rwkv_gate_g1'][j], w['rwkv_gate_g2'][j], w['rwkv_k_k'][j], w['rwkv_k_a'][j],
                            w['rwkv_r_k'][j], w['rwkv_gn_g'][j], w['rwkv_gn_b'][j])
        else:
            x = _ssd_layer(x, ng, sc1, sh1, g1, w['ssm_w_in'][j], w['ssm_conv_w'][j], w['ssm_conv_b'][j],
                           w['ssm_dt_bias'][j], w['ssm_a_log'][j], w['ssm_d'][j], w['ssm_norm_g'][j],
                           w['ssm_w_out'][j])
        x = _moe(x, w['norm_ffn_g'][i], sc2, sh2, g2, wrt, rb, w['moe_w_gate'][i].astype(BF16),
                 w['moe_w_up'][i].astype(BF16), w['moe_w_down'][i].astype(BF16),
                 final_g=w['final_norm_g'] if i == DEPTH - 1 else None)
    return x


def kernel(x_prompt, x_sample, c_prompt, c_sample,
           ada_w, ada_b, norm_mix_g, norm_ffn_g, final_norm_g,
           ret_w_in, ret_gn_g, ret_gn_b, ret_w_out,
           rwkv_mu, rwkv_w_r, rwkv_w_k, rwkv_w_v, rwkv_w_o,
           rwkv_decay_w0, rwkv_decay_w1, rwkv_decay_w2,
           rwkv_iclr_a0, rwkv_iclr_a1, rwkv_iclr_a2,
           rwkv_gate_g1, rwkv_gate_g2, rwkv_k_k, rwkv_k_a, rwkv_r_k, rwkv_gn_g, rwkv_gn_b,
           ssm_w_in, ssm_conv_w, ssm_conv_b, ssm_dt_bias, ssm_a_log, ssm_d, ssm_norm_g, ssm_w_out,
           moe_w_router, moe_router_bias, moe_w_gate, moe_w_up, moe_w_down):
    w = dict(
        norm_mix_g=norm_mix_g, norm_ffn_g=norm_ffn_g, final_norm_g=final_norm_g,
        ret_w_in=ret_w_in, ret_gn_g=ret_gn_g, ret_gn_b=ret_gn_b, ret_w_out=ret_w_out,
        rwkv_mu=rwkv_mu, rwkv_w_r=rwkv_w_r, rwkv_w_k=rwkv_w_k, rwkv_w_v=rwkv_w_v, rwkv_w_o=rwkv_w_o,
        rwkv_decay_w0=rwkv_decay_w0, rwkv_decay_w1=rwkv_decay_w1, rwkv_decay_w2=rwkv_decay_w2,
        rwkv_iclr_a0=rwkv_iclr_a0, rwkv_iclr_a1=rwkv_iclr_a1, rwkv_iclr_a2=rwkv_iclr_a2,
        rwkv_gate_g1=rwkv_gate_g1, rwkv_gate_g2=rwkv_gate_g2, rwkv_k_k=rwkv_k_k, rwkv_k_a=rwkv_k_a,
        rwkv_r_k=rwkv_r_k, rwkv_gn_g=rwkv_gn_g, rwkv_gn_b=rwkv_gn_b,
        ssm_w_in=ssm_w_in, ssm_conv_w=ssm_conv_w, ssm_conv_b=ssm_conv_b, ssm_dt_bias=ssm_dt_bias,
        ssm_a_log=ssm_a_log, ssm_d=ssm_d, ssm_norm_g=ssm_norm_g, ssm_w_out=ssm_w_out,
        moe_w_router=moe_w_router, moe_router_bias=moe_router_bias, moe_w_gate=moe_w_gate,
        moe_w_up=moe_w_up, moe_w_down=moe_w_down,
    )
    nb = x_prompt.shape[0]
    mod = _ada_mod(jnp.concatenate([c_prompt, c_sample], axis=0), ada_w, ada_b)
    y_prompt = _run_trunk(x_prompt, mod[:, :nb], w)
    y_sample = _run_trunk(x_sample, mod[:, nb:], w)
    return (y_prompt, y_sample)
```
